```python
import jax
import jax.numpy as jnp
from jax import lax
import numpy as np


D_MODEL = 1024
BATCH = 4
SEQ = 4096
DEPTH = 2

HEAD_DIM = 64
A_HEADS = 4
IDX_HEADS = 8
IDX_DIM = 32
DSA_TOPK_MAX = 256
B_HEADS = 4
B_KV_HEADS = 2
WINDOW = 128
C_HEADS = 4
MOBA_BLOCK = 256
MOBA_TOPK = 3
MOBA_Q_CHUNK = 64
D_HEADS = 4
Q_BLOCK = 128
N_BRANCH = 4
BRANCH_WIDTH = 4 * HEAD_DIM
PEER_HEADS = 8
N_KEYS = 128
N_EXPERTS = N_KEYS * N_KEYS
PEER_QDIM = 128
PEER_TOPK = 16
PEER_CHUNK = 128
LN_EPS = 1e-5
NEG_INF = -1e30
DEEPNORM_ALPHA = (2 * DEPTH) ** 0.25
DEEPNORM_BETA = (8 * DEPTH) ** -0.25

IN_WIDTHS = (
    A_HEADS * HEAD_DIM, HEAD_DIM, HEAD_DIM,
    IDX_HEADS * IDX_DIM, IDX_DIM, IDX_HEADS,
    B_HEADS * HEAD_DIM, B_KV_HEADS * HEAD_DIM, B_KV_HEADS * HEAD_DIM,
    C_HEADS * HEAD_DIM, C_HEADS * HEAD_DIM, C_HEADS * HEAD_DIM,
    D_HEADS * HEAD_DIM, D_HEADS * HEAD_DIM, D_HEADS * HEAD_DIM,
    N_BRANCH * D_MODEL,
)
IN_COLS = sum(IN_WIDTHS)
V_COLUMN_GROUPS = (2, 8, 11, 14)

kernel_name = 'hybrid_dsa_swa_moba_stickbreak_peer'


def layer_norm(x, g, b):
    xf = x.astype(jnp.float32)
    mu = jnp.mean(xf, axis=-1, keepdims=True)
    var = jnp.mean(jnp.square(xf - mu), axis=-1, keepdims=True)
    return ((xf - mu) * lax.rsqrt(var + LN_EPS)).astype(g.dtype) * g + b


def alibi_slopes():
    n = A_HEADS + B_HEADS + C_HEADS
    s = 2.0 ** (-8.0 * jnp.arange(1, n + 1, dtype=jnp.float32) / n)
    return s[0::3], s[1::3], s[2::3]


def dsa_attention(q, k, v, q_idx, k_idx, w_idx, slopes):
    bn, s_len = q.shape[:2]
    n_sel = min(DSA_TOPK_MAX, s_len // 4)
    pos_k = jnp.arange(s_len)
    idx_scale = (IDX_DIM * IDX_HEADS) ** -0.5
    k_idx32 = k_idx.astype(jnp.float32)
    k32 = k.astype(jnp.float32)
    v32 = v.astype(jnp.float32)
    gather = jax.vmap(lambda t, ix: t[ix])

    def block(i):
        t0 = i * Q_BLOCK
        pos_q = t0 + jnp.arange(Q_BLOCK)
        qi = lax.dynamic_slice_in_dim(q_idx, t0, Q_BLOCK, 1).astype(jnp.float32)
        wi = lax.dynamic_slice_in_dim(w_idx, t0, Q_BLOCK, 1).astype(jnp.float32)
        qa = lax.dynamic_slice_in_dim(q, t0, Q_BLOCK, 1).astype(jnp.float32)
        rel = jax.nn.relu(jnp.einsum('bqhd,bsd->bqhs', qi, k_idx32))
        score = jnp.einsum('bqhs,bqh->bqs', rel, wi) * idx_scale
        score = jnp.where(pos_k[None, :] <= pos_q[:, None], score, -jnp.inf)
        _, sel = lax.top_k(score, n_sel)
        valid = sel <= pos_q[None, :, None]
        kg = gather(k32, sel)
        vg = gather(v32, sel)
        logits = jnp.einsum('bqhd,bqnd->bhqn', qa, kg) * (HEAD_DIM ** -0.5)
        dist = (pos_q[None, :, None] - sel).astype(jnp.float32)
        logits = logits - slopes[None, :, None, None] * dist[:, None]
        logits = jnp.where(valid[:, None], logits, NEG_INF)
        p = jax.nn.softmax(logits, axis=-1)
        return jnp.einsum('bhqn,bqnd->bqhd', p, vg)

    out = lax.map(block, jnp.arange(s_len // Q_BLOCK))
    return jnp.moveaxis(out, 0, 1).reshape(bn, s_len, A_HEADS * HEAD_DIM)


def swa_sink_attention(q, k, v, sinks, slopes):
    bn, s_len = q.shape[:2]
    w = WINDOW
    nb = s_len // w
    grp = B_HEADS // B_KV_HEADS
    qb = q.astype(jnp.float32).reshape(bn, nb, w, B_KV_HEADS, grp, HEAD_DIM)
    kb = k.astype(jnp.float32).reshape(bn, nb, w, B_KV_HEADS, HEAD_DIM)
    vb = v.astype(jnp.float32).reshape(bn, nb, w, B_KV_HEADS, HEAD_DIM)
    shift = lambda t: jnp.pad(t, ((0, 0), (1, 0), (0, 0), (0, 0), (0, 0)))[:, :-1]
    kk = jnp.concatenate([shift(kb), kb], axis=2)
    vv = jnp.concatenate([shift(vb), vb], axis=2)
    logits = jnp.einsum('bnqkgd,bnskd->bnkgqs', qb, kk) * (HEAD_DIM ** -0.5)
    i = jnp.arange(w)[:, None]
    j = jnp.arange(2 * w)[None, :]
    dist = w + i - j
    band = (dist >= 0) & (dist < WINDOW)
    has_prev = jnp.arange(nb)[:, None, None] > 0
    valid = band[None] & (has_prev | (j >= w)[None])
    sl = slopes.reshape(B_KV_HEADS, grp)[:, :, None, None]
    logits = logits - sl * dist.astype(jnp.float32)
    logits = jnp.where(valid[None, :, None, None], logits, NEG_INF)
    sink = jnp.broadcast_to(sinks.astype(jnp.float32).reshape(B_KV_HEADS, grp, 1, 1), logits.shape[:-1] + (1,))
    p = jax.nn.softmax(jnp.concatenate([logits, sink], axis=-1), axis=-1)[..., :-1]
    o = jnp.einsum('bnkgqs,bnskd->bnqkgd', p, vv)
    return o.reshape(bn, s_len, B_HEADS * HEAD_DIM)


def moba_attention(q, k, v, slopes):
    bn, s_len, nh, hd = q.shape
    nblk = -(-s_len // MOBA_BLOCK)
    s_pad = nblk * MOBA_BLOCK
    n_top = min(MOBA_TOPK, nblk - 1)
    pad = ((0, 0), (0, s_pad - s_len), (0, 0), (0, 0))
    kp = jnp.pad(k.astype(jnp.float32), pad)
    vp = jnp.pad(v.astype(jnp.float32), pad)
    k_blk = kp.reshape(bn, nblk, MOBA_BLOCK, nh, hd).transpose(0, 3, 1, 2, 4)
    v_blk = vp.reshape(bn, nblk, MOBA_BLOCK, nh, hd).transpose(0, 3, 1, 2, 4)
    k_mean = jnp.mean(k_blk, axis=3)
    scale = hd ** -0.5
    offs = jnp.arange(MOBA_BLOCK)
    gather = jax.vmap(jax.vmap(lambda tb, ix: tb[ix]))

    def chunk(i):
        t0 = i * MOBA_Q_CHUNK
        pos_q = t0 + jnp.arange(MOBA_Q_CHUNK)
        own = t0 // MOBA_BLOCK
        qc = lax.dynamic_slice_in_dim(q, t0, MOBA_Q_CHUNK, 1).astype(jnp.float32)
        k_own = lax.dynamic_slice_in_dim(kp, own * MOBA_BLOCK, MOBA_BLOCK, 1)
        v_own = lax.dynamic_slice_in_dim(vp, own * MOBA_BLOCK, MOBA_BLOCK, 1)
        pos_own = own * MOBA_BLOCK + offs
        lo = jnp.einsum('bqhd,bshd->bhqs', qc, k_own) * scale
        lo = lo - slopes[:, None, None] * (pos_q[:, None] - pos_own[None, :]).astype(jnp.float32)
        lo = jnp.where(pos_own[None, :] <= pos_q[:, None], lo, NEG_INF)
        if n_top == 0:
            p_own = jax.nn.softmax(lo, axis=-1)
            return jnp.einsum('bhqs,bshd->bqhd', p_own, v_own)
        gate = jnp.einsum('bqhd,bhnd->bhqn', qc, k_mean)
        gate = jnp.where(jnp.arange(nblk) < own, gate, -jnp.inf)
        _, sel = lax.top_k(gate, n_top)
        ks = gather(k_blk, sel)
        vs = gather(v_blk, sel)
        pos_sel = sel[..., None] * MOBA_BLOCK + offs
        ls = jnp.einsum('bqhd,bhqnsd->bhqns', qc, ks) * scale
        ls = ls - slopes[None, :, None, None, None] * (pos_q[None, None, :, None, None] - pos_sel).astype(jnp.float32)
        ls = jnp.where((sel < own)[..., None], ls, NEG_INF)
        ls = ls.reshape(bn, nh, MOBA_Q_CHUNK, n_top * MOBA_BLOCK)
        p = jax.nn.softmax(jnp.concatenate([ls, lo], axis=-1), axis=-1)
        p_sel = p[..., :n_top * MOBA_BLOCK].reshape(bn, nh, MOBA_Q_CHUNK, n_top, MOBA_BLOCK)
        p_own = p[..., n_top * MOBA_BLOCK:]
        return (jnp.einsum('bhqns,bhqnsd->bqhd', p_sel, vs)
                + jnp.einsum('bhqs,bshd->bqhd', p_own, v_own))

    out = lax.map(chunk, jnp.arange(s_len // MOBA_Q_CHUNK))
    return jnp.moveaxis(out, 0, 1).reshape(bn, s_len, nh * hd)


def stick_breaking_attention(q, k, v):
    bn, s_len, nh, hd = q.shape
    k32 = k.astype(jnp.float32)
    v32 = v.astype(jnp.float32)
    pos_k = jnp.arange(s_len)
    scale = hd ** -0.5

    def block(i):
        t0 = i * Q_BLOCK
        pos_q = t0 + jnp.arange(Q_BLOCK)
        qb = lax.dynamic_slice_in_dim(q, t0, Q_BLOCK, 1).astype(jnp.float32)
        z = jnp.einsum('bqhd,bshd->bhqs', qb, k32) * scale
        strict = pos_k[None, :] < pos_q[:, None]
        log_keep = jnp.where(strict, jax.nn.log_sigmoid(-z), 0.0)
        log_after = lax.cumsum(log_keep, axis=3, reverse=True) - log_keep
        a = jnp.where(strict, jnp.exp(jax.nn.log_sigmoid(z) + log_after), 0.0)
        return jnp.einsum('bhqs,bshd->bqhd', a, v32)

    out = lax.map(block, jnp.arange(s_len // Q_BLOCK))
    return jnp.moveaxis(out, 0, 1).reshape(bn, s_len, nh * hd)


def token_mixer_sublayer(x, w_in, w_branch, w_out, sinks, slopes_a, slopes_b, slopes_c):
    bn, s_len, _ = x.shape
    proj = x @ w_in
    offsets = [int(o) for o in np.cumsum(IN_WIDTHS)[:-1]]
    (qa, ka, va, qi, ki, wi, qb, kb, vb, qc, kc, vc, qd, kd, vd, g) = jnp.split(proj, offsets, axis=-1)
    hd = lambda t, h: t.reshape(bn, s_len, h, HEAD_DIM)
    y_a = dsa_attention(hd(qa, A_HEADS), ka, va, qi.reshape(bn, s_len, IDX_HEADS, IDX_DIM), ki, wi, slopes_a)
    y_b = swa_sink_attention(hd(qb, B_HEADS), hd(kb, B_KV_HEADS), hd(vb, B_KV_HEADS), sinks, slopes_b)
    y_c = moba_attention(hd(qc, C_HEADS), hd(kc, C_HEADS), hd(vc, C_HEADS), slopes_c)
    y_d = stick_breaking_attention(hd(qd, D_HEADS), hd(kd, D_HEADS), hd(vd, D_HEADS))
    ys = jnp.stack([y_a, y_b, y_c, y_d], axis=2)
    branches = jnp.einsum('bsnc,ncd->bsnd', ys, w_branch)
    gates = jax.nn.sigmoid(g.reshape(bn, s_len, N_BRANCH, D_MODEL).astype(jnp.float32))
    merged = jnp.sum(gates * branches, axis=2)
    return merged @ w_out


def peer_ffn(x, wq, subkeys, u, v):
    bn, s_len, d = x.shape
    t_len = bn * s_len
    xt = x.reshape(t_len, d)
    q = (xt @ wq).astype(jnp.float32).reshape(t_len, PEER_HEADS, 2, PEER_QDIM // 2)
    sub = jnp.einsum('thpd,hpkd->thpk', q, subkeys.astype(jnp.float32))
    s_top, i_top = lax.top_k(sub, PEER_TOPK)
    cand = s_top[:, :, 0, :, None] + s_top[:, :, 1, None, :]
    sc, ci = lax.top_k(cand.reshape(t_len, PEER_HEADS, PEER_TOPK * PEER_TOPK), PEER_TOPK)
    i1 = jnp.take_along_axis(i_top[:, :, 0], ci // PEER_TOPK, axis=-1)
    i2 = jnp.take_along_axis(i_top[:, :, 1], ci % PEER_TOPK, axis=-1)
    eid = (i1 * N_KEYS + i2).reshape(t_len, PEER_HEADS * PEER_TOPK)
    gate = jax.nn.softmax(sc, axis=-1).reshape(t_len, PEER_HEADS * PEER_TOPK)
    n_chunks = t_len // PEER_CHUNK

    def chunk(args):
        xc, ec, gc = args
        ug = u[ec]
        h = jax.nn.gelu(jnp.einsum('cd,ced->ce', xc, ug).astype(jnp.float32), approximate=False)
        return jnp.einsum('ce,ced->cd', gc * h, v[ec])

    out = lax.map(chunk, (xt.reshape(n_chunks, PEER_CHUNK, d),
                          eid.reshape(n_chunks, PEER_CHUNK, -1),
                          gate.reshape(n_chunks, PEER_CHUNK, -1)))
    return out.reshape(bn, s_len, d)


def setup_inputs(seed: int = 0) -> dict:
    key = jax.random.key(seed)
    ks = jax.random.split(key, 13)
    nrm = lambda k, shape: jax.random.normal(k, shape, jnp.float32)
    x = nrm(ks[0], (BATCH, SEQ, D_MODEL))
    bounds = np.concatenate([[0], np.cumsum(IN_WIDTHS)])
    col_scale = np.ones((IN_COLS,), np.float32)
    for gi in V_COLUMN_GROUPS:
        col_scale[bounds[gi]:bounds[gi + 1]] = DEEPNORM_BETA
    w_in = nrm(ks[1], (DEPTH, D_MODEL, IN_COLS)) * (D_MODEL ** -0.5) * jnp.asarray(col_scale)
    w_branch = nrm(ks[2], (DEPTH, N_BRANCH, BRANCH_WIDTH, D_MODEL)) * (BRANCH_WIDTH ** -0.5 * DEEPNORM_BETA)
    w_out = nrm(ks[3], (DEPTH, D_MODEL, D_MODEL)) * (D_MODEL ** -0.5 * DEEPNORM_BETA)
    attn_sinks = 0.5 * nrm(ks[4], (DEPTH, B_HEADS))
    ln1_g = 1.0 + 0.02 * nrm(ks[5], (DEPTH, D_MODEL))
    ln1_b = 0.02 * nrm(ks[6], (DEPTH, D_MODEL))
    peer_wq = nrm(ks[7], (DEPTH, D_MODEL, PEER_HEADS * PEER_QDIM)) * (D_MODEL ** -0.5)
    peer_subkeys = nrm(ks[8], (DEPTH, PEER_HEADS, 2, N_KEYS, PEER_QDIM // 2)) * ((PEER_QDIM // 2) ** -0.5)
    peer_u = nrm(ks[9], (DEPTH, N_EXPERTS, D_MODEL)) * (D_MODEL ** -0.5 * DEEPNORM_BETA)
    peer_v = nrm(ks[10], (DEPTH, N_EXPERTS, D_MODEL)) * ((PEER_HEADS * PEER_TOPK) ** -0.5 * DEEPNORM_BETA)
    ln2_g = 1.0 + 0.02 * nrm(ks[11], (DEPTH, D_MODEL))
    ln2_b = 0.02 * nrm(ks[12], (DEPTH, D_MODEL))
    return {'x': x, 'w_in': w_in, 'w_branch': w_branch, 'w_out': w_out, 'attn_sinks': attn_sinks,
            'ln1_g': ln1_g, 'ln1_b': ln1_b, 'peer_wq': peer_wq, 'peer_subkeys': peer_subkeys,
            'peer_u': peer_u, 'peer_v': peer_v, 'ln2_g': ln2_g, 'ln2_b': ln2_b}


def reference(x, w_in, w_branch, w_out, attn_sinks, ln1_g, ln1_b, peer_wq, peer_subkeys,
              peer_u, peer_v, ln2_g, ln2_b):
    slopes_a, slopes_b, slopes_c = alibi_slopes()
    h = x
    for l in range(DEPTH):
        mix = token_mixer_sublayer(h, w_in[l], w_branch[l], w_out[l], attn_sinks[l],
                                   slopes_a, slopes_b, slopes_c)
        h = layer_norm(DEEPNORM_ALPHA * h + mix, ln1_g[l], ln1_b[l])
        ffn = peer_ffn(h, peer_wq[l], peer_subkeys[l], peer_u[l], peer_v[l])
        h = layer_norm(DEEPNORM_ALPHA * h + ffn, ln2_g[l], ln2_b[l])
    return h
```

```python
import functools

import jax
import jax.numpy as jnp
import numpy as np
from jax import lax
from jax.experimental import pallas as pl
from jax.experimental.pallas import tpu as pltpu

F32 = jnp.float32
BF16 = jnp.bfloat16
I32 = jnp.int32

D_MODEL = 1024
DEPTH = 2
HEAD_DIM = 64
A_HEADS = 4
IDX_HEADS = 8
IDX_DIM = 32
DSA_TOPK_MAX = 256
B_HEADS = 4
B_KV_HEADS = 2
WINDOW = 128
C_HEADS = 4
MOBA_BLOCK = 256
MOBA_TOPK = 3
D_HEADS = 4
Q_BLOCK = 128
N_BRANCH = 4
BRANCH_WIDTH = 4 * HEAD_DIM
PEER_HEADS = 8
N_KEYS = 128
N_EXPERTS = N_KEYS * N_KEYS
PEER_QDIM = 128
PEER_TOPK = 16
LN_EPS = 1e-5
NEG_INF = -1e30
DEEPNORM_ALPHA = (2 * DEPTH) ** 0.25
IDX_SCALE = (IDX_DIM * IDX_HEADS) ** -0.5
QK_SCALE = HEAD_DIM ** -0.5

IN_WIDTHS = (
    A_HEADS * HEAD_DIM, HEAD_DIM, HEAD_DIM,
    IDX_HEADS * IDX_DIM, IDX_DIM, IDX_HEADS,
    B_HEADS * HEAD_DIM, B_KV_HEADS * HEAD_DIM, B_KV_HEADS * HEAD_DIM,
    C_HEADS * HEAD_DIM, C_HEADS * HEAD_DIM, C_HEADS * HEAD_DIM,
    D_HEADS * HEAD_DIM, D_HEADS * HEAD_DIM, D_HEADS * HEAD_DIM,
    N_BRANCH * D_MODEL,
)

V7X_LANES = 128
V7X_VMEM_BYTES = 64 * 1024 * 1024
VMEM_LIMIT_BYTES = (V7X_VMEM_BYTES * 3) // 4

INT_MIN = -(2 ** 31)

_NT = (((1,), (1,)), ((), ()))


def _alibi_slopes():
    n = A_HEADS + B_HEADS + C_HEADS
    s = [2.0 ** (-8.0 * k / n) for k in range(1, n + 1)]
    return tuple(s[0::3]), tuple(s[1::3]), tuple(s[2::3])


def _dot(a, b):
    return jnp.dot(a, b, preferred_element_type=F32)


def _dot_nt(a, b):
    return lax.dot_general(a, b, _NT, preferred_element_type=F32)


def _split_hi_lo(x):
    hi = x.astype(BF16)
    lo = (x - hi.astype(F32)).astype(BF16)
    return hi, lo


def _params(*sem):
    return pltpu.CompilerParams(dimension_semantics=sem, vmem_limit_bytes=VMEM_LIMIT_BYTES)


def _proj_kernel(kinds, x_ref, *refs):
    n_w = sum(1 if k == "bf16" else 2 for k in kinds)
    w_refs, o_refs = refs[:n_w], refs[n_w:]
    xh, xl = _split_hi_lo(x_ref[...])
    wi = oi = 0
    for k in kinds:
        if k == "bf16":
            o_refs[oi][...] = _dot(xh, w_refs[wi][...]).astype(BF16)
            wi += 1
            oi += 1
            continue
        wh, wl = w_refs[wi][...], w_refs[wi + 1][...]
        wi += 2
        acc = _dot(xh, wh) + (_dot(xl, wh) + _dot(xh, wl))
        if k == "f32":
            o_refs[oi][...] = acc
            oi += 1
        else:
            hi, lo = _split_hi_lo(acc)
            o_refs[oi][...] = hi
            o_refs[oi + 1][...] = lo
            oi += 2


def _project(x2d, weights, kinds, tm):
    t_len, kdim = x2d.shape
    ins, in_specs, out_shapes, out_specs = [x2d], [pl.BlockSpec((tm, kdim), lambda i: (i, 0))], [], []
    for w, k in zip(weights, kinds):
        n = w.shape[1]
        wspec = pl.BlockSpec((kdim, n), lambda i: (0, 0))
        ospec = pl.BlockSpec((tm, n), lambda i: (i, 0))
        if k == "bf16":
            ins.append(w.astype(BF16))
            in_specs.append(wspec)
            out_shapes.append(jax.ShapeDtypeStruct((t_len, n), BF16))
            out_specs.append(ospec)
        else:
            wh, wl = _split_hi_lo(w)
            ins += [wh, wl]
            in_specs += [wspec, wspec]
            if k == "f32":
                out_shapes.append(jax.ShapeDtypeStruct((t_len, n), F32))
                out_specs.append(ospec)
            else:
                out_shapes += [jax.ShapeDtypeStruct((t_len, n), BF16)] * 2
                out_specs += [ospec, ospec]
    return pl.pallas_call(
        functools.partial(_proj_kernel, tuple(kinds)),
        grid=(t_len // tm,),
        in_specs=in_specs, out_specs=out_specs, out_shape=out_shapes,
        compiler_params=_params("arbitrary"),
        name="in_proj",
    )(*ins)


def _dsa_kernel(qi3_ref, ki3_ref, wi_ref, qa_ref, ka_ref, va_ref, o_ref,
                keys_sc, wb_sc, m_sc, l_sc, acc_sc, *, n_sel, slopes):
    qb = Q_BLOCK
    i = pl.program_id(1)
    nch = i + 1
    row = lax.broadcasted_iota(I32, (qb, qb), 0)
    col = lax.broadcasted_iota(I32, (qb, qb), 1)

    wi = wi_ref[...]
    for h in range(IDX_HEADS):
        wb_sc[h] = jnp.broadcast_to(wi[:, h:h + 1], (qb, qb))
    q3 = qi3_ref[...].reshape(IDX_HEADS * qb, 3 * IDX_DIM)

    def score_chunk(c, carry):
        k3 = ki3_ref[pl.ds(pl.multiple_of(c * qb, qb), qb), :]
        r = jnp.maximum(_dot_nt(q3, k3), 0.0)
        sc = r[0:qb] * wb_sc[0]
        for h in range(1, IDX_HEADS):
            sc = sc + r[h * qb:(h + 1) * qb] * wb_sc[h]
        sc = sc * IDX_SCALE
        sc = jnp.where(sc == 0.0, 0.0, sc)
        bits = lax.bitcast_convert_type(sc, I32)
        key = bits ^ ((bits >> 31) & 0x7FFFFFFF)
        valid = ((c - i) * qb + col) <= row
        keys_sc[c] = jnp.where(valid, key, INT_MIN)
        return carry

    lax.fori_loop(0, nch, score_chunk, 0)

    def count(pred):
        def body(c, acc):
            return acc + jnp.where(pred(keys_sc[c]), 1.0, 0.0)
        acc = lax.fori_loop(0, nch, body, jnp.zeros((qb, qb), F32))
        return jnp.sum(acc, axis=1, keepdims=True)

    def bit_step(it, tau):
        cand = tau ^ lax.shift_left(jnp.int32(1), 31 - it)
        cand_b = jnp.broadcast_to(cand, (qb, qb))
        cnt = count(lambda k: k >= cand_b)
        return jnp.where(cnt >= n_sel, cand, tau)

    tau = lax.fori_loop(0, 32, bit_step, jnp.full((qb, 1), INT_MIN, I32))
    tau_b = jnp.broadcast_to(tau, (qb, qb))
    need = n_sel - count(lambda k: k > tau_b)

    m_sc[...] = jnp.full(m_sc.shape, NEG_INF, F32)
    l_sc[...] = jnp.zeros(l_sc.shape, F32)
    acc_sc[...] = jnp.zeros(acc_sc.shape, F32)
    before = jnp.where(row < col, 1.0, 0.0).astype(BF16)
    qa = qa_ref[...].reshape(A_HEADS * qb, HEAD_DIM) * QK_SCALE

    def attend_chunk(c, n_eq_before):
        key = keys_sc[c]
        eq = key == tau_b
        eqf = jnp.where(eq, 1.0, 0.0)
        prefix = _dot(eqf.astype(BF16), before)
        take_eq = eq & ((n_eq_before + prefix) < need)
        valid = ((c - i) * qb + col) <= row
        sel = ((key > tau_b) | take_eq) & valid
        start = pl.multiple_of(c * qb, qb)
        k = ka_ref[pl.ds(start, qb), :]
        v = va_ref[pl.ds(start, qb), :]
        s_all = _dot_nt(qa, k)
        dist = ((i - c) * qb + row - col).astype(F32)
        for h in range(A_HEADS):
            s = jnp.where(sel, s_all[h * qb:(h + 1) * qb] - slopes[h] * dist, NEG_INF)
            m_prev = m_sc[h]
            m_next = jnp.maximum(m_prev, jnp.max(s, axis=1, keepdims=True))
            p = jnp.exp(s - m_next)
            alpha = jnp.exp(m_prev - m_next)
            l_sc[h] = alpha * l_sc[h] + jnp.sum(p, axis=1, keepdims=True)
            m_sc[h] = m_next
            acc_sc[h] = acc_sc[h] * alpha[:, :HEAD_DIM] + _dot(p.astype(BF16), v)
        return n_eq_before + jnp.sum(eqf, axis=1, keepdims=True)

    lax.fori_loop(0, nch, attend_chunk, jnp.zeros((qb, 1), F32))
    for h in range(A_HEADS):
        o_ref[h] = (acc_sc[h] / l_sc[h][:, :HEAD_DIM]).astype(o_ref.dtype)


def _dsa(qi3, ki3, wi, qa, ka, va, slopes):
    bn, _, s_len, _ = qa.shape
    nq = s_len // Q_BLOCK
    n_sel = min(DSA_TOPK_MAX, s_len // 4)
    qspec = lambda h, d: pl.BlockSpec((None, h, Q_BLOCK, d), lambda b, i: (b, 0, i, 0))
    kspec = lambda d: pl.BlockSpec((None, s_len, d), lambda b, i: (b, 0, 0))
    return pl.pallas_call(
        functools.partial(_dsa_kernel, n_sel=float(n_sel), slopes=slopes),
        grid=(bn, nq),
        in_specs=[qspec(IDX_HEADS, 3 * IDX_DIM), kspec(3 * IDX_DIM),
                  pl.BlockSpec((None, Q_BLOCK, IDX_HEADS), lambda b, i: (b, i, 0)),
                  qspec(A_HEADS, HEAD_DIM), kspec(HEAD_DIM), kspec(HEAD_DIM)],
        out_specs=qspec(A_HEADS, HEAD_DIM),
        out_shape=jax.ShapeDtypeStruct((bn, A_HEADS, s_len, HEAD_DIM), BF16),
        scratch_shapes=[pltpu.VMEM((nq, Q_BLOCK, Q_BLOCK), I32),
                        pltpu.VMEM((IDX_HEADS, Q_BLOCK, Q_BLOCK), F32),
                        pltpu.VMEM((A_HEADS, Q_BLOCK, Q_BLOCK), F32),
                        pltpu.VMEM((A_HEADS, Q_BLOCK, Q_BLOCK), F32),
                        pltpu.VMEM((A_HEADS, Q_BLOCK, HEAD_DIM), F32)],
        compiler_params=_params("arbitrary", "arbitrary"),
        name="dsa",
    )(qi3, ki3, wi, qa, ka, va)


def _swa_kernel(sink_ref, q_ref, kp_ref, kc_ref, vp_ref, vc_ref, o_ref, *, slopes):
    w = WINDOW
    i = pl.program_id(1)
    grp = B_HEADS // B_KV_HEADS
    row = lax.broadcasted_iota(I32, (w, w), 0)
    col = lax.broadcasted_iota(I32, (w, w), 1)
    valid_p = jnp.logical_and(col > row, i > 0)
    valid_c = col <= row
    dist_p = (w + row - col).astype(F32)
    dist_c = (row - col).astype(F32)
    for kh in range(B_KV_HEADS):
        q2 = q_ref[kh * grp:(kh + 1) * grp].reshape(grp * w, HEAD_DIM) * QK_SCALE
        sp = _dot_nt(q2, kp_ref[kh])
        sc = _dot_nt(q2, kc_ref[kh])
        for g in range(grp):
            h = kh * grp + g
            lp = jnp.where(valid_p, sp[g * w:(g + 1) * w] - slopes[h] * dist_p, NEG_INF)
            lc = jnp.where(valid_c, sc[g * w:(g + 1) * w] - slopes[h] * dist_c, NEG_INF)
            sink = sink_ref[h]
            m = jnp.maximum(jnp.maximum(jnp.max(lp, axis=1, keepdims=True),
                                        jnp.max(lc, axis=1, keepdims=True)), sink)
            pp = jnp.exp(lp - m)
            pc = jnp.exp(lc - m)
            den = (jnp.sum(pp, axis=1, keepdims=True) + jnp.sum(pc, axis=1, keepdims=True)
                   + jnp.exp(sink - m))
            o = _dot(pp.astype(BF16), vp_ref[kh]) + _dot(pc.astype(BF16), vc_ref[kh])
            o_ref[h] = (o / den).astype(o_ref.dtype)


def _swa(sinks, q, k, v, slopes):
    bn, _, s_len, _ = q.shape
    nb = s_len // WINDOW
    cur = pl.BlockSpec((None, B_KV_HEADS, WINDOW, HEAD_DIM), lambda b, i: (b, 0, i, 0))
    prev = pl.BlockSpec((None, B_KV_HEADS, WINDOW, HEAD_DIM), lambda b, i: (b, 0, jnp.maximum(i - 1, 0), 0))
    qspec = pl.BlockSpec((None, B_HEADS, WINDOW, HEAD_DIM), lambda b, i: (b, 0, i, 0))
    return pl.pallas_call(
        functools.partial(_swa_kernel, slopes=slopes),
        grid=(bn, nb),
        in_specs=[pl.BlockSpec(memory_space=pltpu.SMEM), qspec, prev, cur, prev, cur],
        out_specs=qspec,
        out_shape=jax.ShapeDtypeStruct((bn, B_HEADS, s_len, HEAD_DIM), BF16),
        compiler_params=_params("arbitrary", "arbitrary"),
        name="swa",
    )(sinks, q, k, k, v, v)


def _moba_kernel(qh_ref, ql_ref, k_ref, v_ref, o_ref,
                 kmh_sc, kml_sc, sel_sc, m_sc, l_sc, acc_sc, *, nblk, n_top, slopes):
    mb = MOBA_BLOCK
    i = pl.program_id(1)
    row = lax.broadcasted_iota(I32, (mb, mb), 0)
    col = lax.broadcasted_iota(I32, (mb, mb), 1)

    @pl.when(i == 0)
    def _():
        for h in range(C_HEADS):
            km = jnp.sum(k_ref[h].reshape(nblk, mb, HEAD_DIM), axis=1) * (1.0 / mb)
            kmh, kml = _split_hi_lo(km)
            kmh_sc[h] = kmh
            kml_sc[h] = kml

    blk = lax.broadcasted_iota(I32, (mb, nblk), 1)
    for h in range(C_HEADS):
        qh, ql = qh_ref[h], ql_ref[h]
        gate = _dot_nt(qh, kmh_sc[h]) + (_dot_nt(ql, kmh_sc[h]) + _dot_nt(qh, kml_sc[h]))
        rank = jnp.zeros((mb, nblk), F32)
        for j in range(nblk):
            gj = gate[:, j:j + 1]
            beats = (gj > gate) | ((gj == gate) & (j < blk))
            rank = rank + jnp.where(jnp.logical_and(beats, j < i), 1.0, 0.0)
        sel_sc[h] = jnp.where((rank < n_top) & (blk < i), 1.0, 0.0)

    m_sc[...] = jnp.full(m_sc.shape, NEG_INF, F32)
    l_sc[...] = jnp.zeros(l_sc.shape, F32)
    acc_sc[...] = jnp.zeros(acc_sc.shape, F32)

    def attend(j, mask_of_head):
        start = pl.multiple_of(j * mb, mb)
        dist = ((i - j) * mb + row - col).astype(F32)
        for h in range(C_HEADS):
            k = k_ref[h, pl.ds(start, mb), :].astype(BF16)
            v = v_ref[h, pl.ds(start, mb), :]
            s = _dot_nt(qh_ref[h] * QK_SCALE, k) - slopes[h] * dist
            s = jnp.where(mask_of_head(h), s, NEG_INF)
            m_prev = m_sc[h]
            m_next = jnp.maximum(m_prev, jnp.max(s, axis=1, keepdims=True))
            p = jnp.exp(s - m_next)
            alpha = jnp.exp(m_prev - m_next)
            l_sc[h] = alpha * l_sc[h] + jnp.sum(p, axis=1, keepdims=True)
            m_sc[h] = m_next
            acc_sc[h] = acc_sc[h] * alpha[:, :HEAD_DIM] + _dot(p.astype(BF16), v)

    def past_block(j, carry):
        onehot = jnp.where(lax.broadcasted_iota(I32, (nblk, mb), 0) == j, 1.0, 0.0).astype(BF16)
        attend(j, lambda h: _dot(sel_sc[h].astype(BF16), onehot) > 0.5)
        return carry

    lax.fori_loop(0, i, past_block, 0)
    attend(i, lambda h: col <= row)
    for h in range(C_HEADS):
        o_ref[h] = (acc_sc[h] / l_sc[h][:, :HEAD_DIM]).astype(o_ref.dtype)


def _moba(qh, ql, k, v, slopes):
    bn, _, s_len, _ = qh.shape
    nblk = s_len // MOBA_BLOCK
    n_top = min(MOBA_TOPK, nblk - 1)
    qspec = pl.BlockSpec((None, C_HEADS, MOBA_BLOCK, HEAD_DIM), lambda b, i: (b, 0, i, 0))
    kspec = pl.BlockSpec((None, C_HEADS, s_len, HEAD_DIM), lambda b, i: (b, 0, 0, 0))
    return pl.pallas_call(
        functools.partial(_moba_kernel, nblk=nblk, n_top=float(n_top), slopes=slopes),
        grid=(bn, nblk),
        in_specs=[qspec, qspec, kspec, kspec],
        out_specs=qspec,
        out_shape=jax.ShapeDtypeStruct((bn, C_HEADS, s_len, HEAD_DIM), BF16),
        scratch_shapes=[pltpu.VMEM((C_HEADS, nblk, HEAD_DIM), BF16),
                        pltpu.VMEM((C_HEADS, nblk, HEAD_DIM), BF16),
                        pltpu.VMEM((C_HEADS, MOBA_BLOCK, nblk), F32),
                        pltpu.VMEM((C_HEADS, MOBA_BLOCK, MOBA_BLOCK), F32),
                        pltpu.VMEM((C_HEADS, MOBA_BLOCK, MOBA_BLOCK), F32),
                        pltpu.VMEM((C_HEADS, MOBA_BLOCK, HEAD_DIM), F32)],
        compiler_params=_params("arbitrary", "arbitrary"),
        name="moba",
    )(qh, ql, k, v)


def _stick_kernel(q_ref, k_ref, v_ref, o_ref):
    qb = Q_BLOCK
    i = pl.program_id(2)
    row = lax.broadcasted_iota(I32, (qb, qb), 0)
    col = lax.broadcasted_iota(I32, (qb, qb), 1)
    after = jnp.where(row > col, 1.0, 0.0).astype(BF16)
    q = q_ref[...] * QK_SCALE

    def tile(it, carry):
        later, acc = carry
        j = i - it
        start = pl.multiple_of(j * qb, qb)
        z = _dot_nt(q, k_ref[pl.ds(start, qb), :])
        strict = ((j - i) * qb + col) < row
        t = jnp.log1p(jnp.exp(-jnp.abs(z)))
        log_keep = jnp.where(strict, -(jnp.maximum(z, 0.0) + t), 0.0)
        log_beta = jnp.minimum(z, 0.0) - t
        lk_hi, lk_lo = _split_hi_lo(log_keep)
        log_after = later + (_dot(lk_hi, after) + _dot(lk_lo, after))
        a = jnp.where(strict, jnp.exp(log_beta + log_after), 0.0)
        acc = acc + _dot(a.astype(BF16), v_ref[pl.ds(start, qb), :])
        return later + jnp.sum(log_keep, axis=1, keepdims=True), acc

    _, acc = lax.fori_loop(0, i + 1, tile, (jnp.zeros((qb, 1), F32), jnp.zeros((qb, HEAD_DIM), F32)))
    o_ref[...] = acc.astype(o_ref.dtype)


def _stick(q, k, v):
    bn, nh, s_len, _ = q.shape
    qspec = pl.BlockSpec((None, None, Q_BLOCK, HEAD_DIM), lambda b, h, i: (b, h, i, 0))
    kspec = pl.BlockSpec((None, None, s_len, HEAD_DIM), lambda b, h, i: (b, h, 0, 0))
    return pl.pallas_call(
        _stick_kernel,
        grid=(bn, nh, s_len // Q_BLOCK),
        in_specs=[qspec, kspec, kspec],
        out_specs=qspec,
        out_shape=jax.ShapeDtypeStruct((bn, nh, s_len, HEAD_DIM), BF16),
        compiler_params=_params("arbitrary", "arbitrary", "arbitrary"),
        name="stick",
    )(q, k, v)


def _layer_norm(r, g, b):
    mu = jnp.mean(r, axis=-1, keepdims=True)
    d = r - mu
    var = jnp.mean(d * d, axis=-1, keepdims=True)
    return d * lax.rsqrt(var + LN_EPS) * g + b


def _merge_kernel(x_ref, ya_ref, yb_ref, yc_ref, yd_ref, wg_ref, wb_ref, wo_ref, g_ref, b_ref, h_ref):
    x = x_ref[...]
    xb = x.astype(BF16)
    merged = None
    for n, y_ref in enumerate((ya_ref, yb_ref, yc_ref, yd_ref)):
        gate = jax.nn.sigmoid(_dot(xb, wg_ref[:, n * D_MODEL:(n + 1) * D_MODEL]))
        term = gate * _dot(y_ref[...], wb_ref[n])
        merged = term if merged is None else merged + term
    mix = _dot(merged.astype(BF16), wo_ref[...])
    h_ref[...] = _layer_norm(DEEPNORM_ALPHA * x + mix, g_ref[...], b_ref[...])


def _merge(x2d, ys, w_gate, w_branch, w_out, g, b, tm):
    t_len = x2d.shape[0]
    row = lambda n: pl.BlockSpec((tm, n), lambda i: (i, 0))
    full = lambda shape: pl.BlockSpec(shape, lambda i: (0,) * len(shape))
    return pl.pallas_call(
        _merge_kernel,
        grid=(t_len // tm,),
        in_specs=[row(D_MODEL)] + [row(BRANCH_WIDTH)] * N_BRANCH
                 + [full(w_gate.shape), full(w_branch.shape), full(w_out.shape),
                    full((1, D_MODEL)), full((1, D_MODEL))],
        out_specs=row(D_MODEL),
        out_shape=jax.ShapeDtypeStruct((t_len, D_MODEL), F32),
        compiler_params=_params("arbitrary"),
        name="merge_ln",
    )(x2d, *ys, w_gate, w_branch, w_out, g.reshape(1, D_MODEL), b.reshape(1, D_MODEL))


def _top16_rows(s, vals_sc):
    n, tm = s.shape
    rows = lax.broadcasted_iota(I32, (n, tm), 0).astype(F32)
    rank = jnp.full((n, tm), float(PEER_TOPK), F32)
    work = s
    for k in range(PEER_TOPK):
        m = jnp.max(work, axis=0, keepdims=True)
        first = jnp.min(jnp.where(work == m, rows, float(n)), axis=0, keepdims=True)
        hit = rows == first
        rank = jnp.where(hit, float(k), rank)
        work = jnp.where(hit, -jnp.inf, work)
        vals_sc[k:k + 1, :] = m
    return rank


def _peer_route_kernel(h_ref, wqh_ref, wql_ref, wsh_ref, wsl_ref,
                       rank2_ref, r_ref, e2_ref, e1n_ref, a_sc, b_sc, c_sc, top_sc):
    tm = h_ref.shape[0]
    hh, hl = _split_hi_lo(h_ref[...])
    qt = _dot_nt(wqh_ref[...], hh) + (_dot_nt(wql_ref[...], hh) + _dot_nt(wqh_ref[...], hl))
    for h in range(PEER_HEADS):
        qh, ql = _split_hi_lo(qt[h * PEER_QDIM:(h + 1) * PEER_QDIM, :])
        st = _dot(wsh_ref[h], qh) + (_dot(wsl_ref[h], qh) + _dot(wsh_ref[h], ql))
        s1, s2 = st[:N_KEYS], st[N_KEYS:]
        rank1 = _top16_rows(s1, a_sc)
        rank2 = _top16_rows(s2, b_sc)
        a16, b16 = a_sc[...], b_sc[...]
        c_sc[0:16, :] = a16[0:1] + b16
        for ia in range(1, 8):
            c_sc[8 + 8 * ia:16 + 8 * ia, :] = a16[ia:ia + 1] + b16[0:8]
        c_sc[72:80, :] = a16[8:16] + b16[0:1]
        _top16_rows(c_sc[...], top_sc)
        top = top_sc[...]
        tau = top[PEER_TOPK - 1:PEER_TOPK]
        z = jnp.sum(jnp.exp(top - top[0:1]), axis=0, keepdims=True)
        r = jnp.zeros((N_KEYS, tm), F32)
        for j in range(PEER_TOPK):
            r = r + jnp.where((s1 + b16[j:j + 1]) >= tau, 1.0, 0.0)
        r_ref[h] = jnp.where(rank1 < PEER_TOPK, r, 0.0)
        rank2_ref[h] = rank2
        e2_ref[h] = jnp.exp(s2 - b16[0:1])
        e1n_ref[h] = jnp.exp(s1 - a16[0:1]) / z


def _peer_route(h2d, wq_t, wsub_t, tm):
    t_len = h2d.shape[0]
    wqh, wql = _split_hi_lo(wq_t)
    wsh, wsl = _split_hi_lo(wsub_t)
    full = lambda shape: pl.BlockSpec(shape, lambda i: (0,) * len(shape))
    ospec = pl.BlockSpec((PEER_HEADS, N_KEYS, tm), lambda i: (0, 0, i))
    oshape = jax.ShapeDtypeStruct((PEER_HEADS, N_KEYS, t_len), F32)
    return pl.pallas_call(
        _peer_route_kernel,
        grid=(t_len // tm,),
        in_specs=[pl.BlockSpec((tm, D_MODEL), lambda i: (i, 0)),
                  full(wqh.shape), full(wql.shape), full(wsh.shape), full(wsl.shape)],
        out_specs=[ospec] * 4,
        out_shape=[oshape] * 4,
        scratch_shapes=[pltpu.VMEM((PEER_TOPK, tm), F32), pltpu.VMEM((PEER_TOPK, tm), F32),
                        pltpu.VMEM((80, tm), F32), pltpu.VMEM((PEER_TOPK, tm), F32)],
        compiler_params=_params("arbitrary"),
        name="peer_route",
    )(h2d, wqh, wql, wsh, wsl)


def _peer_dense_kernel(h_ref, u_ref, vt_ref, rank2_ref, r_ref, e2_ref, e1n_ref, g_ref, b_ref,
                       o_ref, xb_sc, p_sc, acc_sc, *, te):
    j = pl.program_id(1)
    rows_per_step = te // N_KEYS

    @pl.when(j == 0)
    def _():
        acc_sc[...] = jnp.zeros(acc_sc.shape, F32)
        xb_sc[...] = h_ref[...].astype(BF16)

    ht = _dot_nt(u_ref[...], xb_sc[...])
    act = 0.5 * ht * (1.0 + lax.erf(ht * float(np.sqrt(0.5))))
    for a in range(rows_per_step):
        i1 = j * rows_per_step + a
        gsum = None
        for h in range(PEER_HEADS):
            r_row = r_ref[h, pl.ds(i1, 1), :]
            w_row = e1n_ref[h, pl.ds(i1, 1), :]
            term = jnp.where(rank2_ref[h] < r_row, e2_ref[h] * w_row, 0.0)
            gsum = term if gsum is None else gsum + term
        p_sc[a * N_KEYS:(a + 1) * N_KEYS, :] = (gsum * act[a * N_KEYS:(a + 1) * N_KEYS]).astype(BF16)
    acc_sc[...] += _dot(vt_ref[...], p_sc[...])

    @pl.when(j == pl.num_programs(1) - 1)
    def _():
        r = DEEPNORM_ALPHA * h_ref[...] + acc_sc[...].T
        o_ref[...] = _layer_norm(r, g_ref[...], b_ref[...])


def _peer_dense(h2d, u_bf, vt_bf, rank2, r, e2, e1n, g, b, tm, te):
    t_len = h2d.shape[0]
    rspec = pl.BlockSpec((PEER_HEADS, N_KEYS, tm), lambda i, j: (0, 0, i))
    vec = pl.BlockSpec((1, D_MODEL), lambda i, j: (0, 0))
    return pl.pallas_call(
        functools.partial(_peer_dense_kernel, te=te),
        grid=(t_len // tm, N_EXPERTS // te),
        in_specs=[pl.BlockSpec((tm, D_MODEL), lambda i, j: (i, 0)),
                  pl.BlockSpec((te, D_MODEL), lambda i, j: (j, 0)),
                  pl.BlockSpec((D_MODEL, te), lambda i, j: (0, j)),
                  rspec, rspec, rspec, rspec, vec, vec],
        out_specs=pl.BlockSpec((tm, D_MODEL), lambda i, j: (i, 0)),
        out_shape=jax.ShapeDtypeStruct((t_len, D_MODEL), F32),
        scratch_shapes=[pltpu.VMEM((tm, D_MODEL), BF16), pltpu.VMEM((te, tm), BF16),
                        pltpu.VMEM((D_MODEL, tm), F32)],
        compiler_params=_params("arbitrary", "arbitrary"),
        name="peer_dense",
    )(h2d, u_bf, vt_bf, rank2, r, e2, e1n, g.reshape(1, D_MODEL), b.reshape(1, D_MODEL))


_PROJ_KINDS = ("bf16", "bf16", "bf16", "hilo", "hilo", "f32",
               "bf16", "bf16", "bf16",
               "hilo", "f32", "bf16",
               "bf16", "bf16", "bf16")


def _heads(a, bn, s_len, nh):
    return a.reshape(bn, s_len, nh, a.shape[-1] // nh).transpose(0, 2, 1, 3)


def _unheads(a):
    bn, nh, s_len, d = a.shape
    return a.transpose(0, 2, 1, 3).reshape(bn * s_len, nh * d)


def _sub_key_blocks(subkeys):
    hn, _, nk, d = subkeys.shape
    z = jnp.zeros((hn, nk, d), subkeys.dtype)
    top = jnp.concatenate([subkeys[:, 0], z], axis=-1)
    bot = jnp.concatenate([z, subkeys[:, 1]], axis=-1)
    return jnp.concatenate([top, bot], axis=1)


def _token_mixer(h2d, bn, s_len, w_in, w_branch, w_out, sinks, ln_g, ln_b, slopes):
    offs = np.concatenate([[0], np.cumsum(IN_WIDTHS)])
    cols = [w_in[:, offs[k]:offs[k + 1]] for k in range(len(IN_WIDTHS))]
    (qa, ka, va, qi_h, qi_l, ki_h, ki_l, wi, qb, kb, vb, qc_h, qc_l, kc, vc, qd, kd, vd) = _project(
        h2d, cols[:15], _PROJ_KINDS, tm=512)

    qi_h4 = qi_h.reshape(bn, s_len, IDX_HEADS, IDX_DIM)
    qi_l4 = qi_l.reshape(bn, s_len, IDX_HEADS, IDX_DIM)
    qi3 = jnp.concatenate([qi_h4, qi_l4, qi_h4], axis=-1).transpose(0, 2, 1, 3)
    ki3 = jnp.concatenate([ki_h, ki_h, ki_l], axis=-1).reshape(bn, s_len, 3 * IDX_DIM)
    y_a = _dsa(qi3, ki3, wi.reshape(bn, s_len, IDX_HEADS), _heads(qa, bn, s_len, A_HEADS),
               ka.reshape(bn, s_len, HEAD_DIM), va.reshape(bn, s_len, HEAD_DIM), slopes[0])
    y_b = _swa(sinks, _heads(qb, bn, s_len, B_HEADS), _heads(kb, bn, s_len, B_KV_HEADS),
               _heads(vb, bn, s_len, B_KV_HEADS), slopes[1])
    y_c = _moba(_heads(qc_h, bn, s_len, C_HEADS), _heads(qc_l, bn, s_len, C_HEADS),
                _heads(kc, bn, s_len, C_HEADS), _heads(vc, bn, s_len, C_HEADS), slopes[2])
    y_d = _stick(_heads(qd, bn, s_len, D_HEADS), _heads(kd, bn, s_len, D_HEADS),
                 _heads(vd, bn, s_len, D_HEADS))
    ys = [_unheads(y) for y in (y_a, y_b, y_c, y_d)]
    return _merge(h2d, ys, cols[15].astype(BF16), w_branch.astype(BF16), w_out.astype(BF16),
                  ln_g, ln_b, tm=256)


def _peer(h2d, wq, subkeys, u, v, ln_g, ln_b):
    rank2, r, e2, e1n = _peer_route(h2d, wq.T, _sub_key_blocks(subkeys), tm=256)
    return _peer_dense(h2d, u.astype(BF16), v.T.astype(BF16), rank2, r, e2, e1n, ln_g, ln_b,
                       tm=512, te=512)


def kernel(x, w_in, w_branch, w_out, attn_sinks, ln1_g, ln1_b, peer_wq, peer_subkeys,
           peer_u, peer_v, ln2_g, ln2_b):
    bn, s_len, d = x.shape
    slopes = _alibi_slopes()
    h = x.reshape(bn * s_len, d)
    for l in range(DEPTH):
        h = _token_mixer(h, bn, s_len, w_in[l], w_branch[l], w_out[l], attn_sinks[l],
                         ln1_g[l], ln1_b[l], slopes)
        h = _peer(h, peer_wq[l], peer_subkeys[l], peer_u[l], peer_v[l], ln2_g[l], ln2_b[l])
    return h.reshape(bn, s_len, d)
```

```python
import functools

import jax
import jax.numpy as jnp
import numpy as np
from jax import lax
from jax.experimental import pallas as pl
from jax.experimental.pallas import tpu as pltpu

F32 = jnp.float32
BF16 = jnp.bfloat16
I32 = jnp.int32

D_MODEL = 1024
DEPTH = 2
HEAD_DIM = 64
A_HEADS = 4
IDX_HEADS = 8
IDX_DIM = 32
DSA_TOPK_MAX = 256
B_HEADS = 4
B_KV_HEADS = 2
WINDOW = 128
C_HEADS = 4
MOBA_BLOCK = 256
MOBA_TOPK = 3
D_HEADS = 4
N_BRANCH = 4
BRANCH_WIDTH = 4 * HEAD_DIM
PEER_HEADS = 8
N_KEYS = 128
N_EXPERTS = N_KEYS * N_KEYS
PEER_QDIM = 128
PEER_TOPK = 16
LN_EPS = 1e-5
NEG_INF = -1e30
DEEPNORM_ALPHA = (2 * DEPTH) ** 0.25
IDX_SCALE = (IDX_DIM * IDX_HEADS) ** -0.5
QK_SCALE = HEAD_DIM ** -0.5

IN_WIDTHS = (
    A_HEADS * HEAD_DIM, HEAD_DIM, HEAD_DIM,
    IDX_HEADS * IDX_DIM, IDX_DIM, IDX_HEADS,
    B_HEADS * HEAD_DIM, B_KV_HEADS * HEAD_DIM, B_KV_HEADS * HEAD_DIM,
    C_HEADS * HEAD_DIM, C_HEADS * HEAD_DIM, C_HEADS * HEAD_DIM,
    D_HEADS * HEAD_DIM, D_HEADS * HEAD_DIM, D_HEADS * HEAD_DIM,
    N_BRANCH * D_MODEL,
)

V7X_LANES = 128
V7X_SUBLANES = 8
BF16_ROWS = 2 * V7X_SUBLANES
V7X_VMEM_BYTES = 64 * 1024 * 1024
VMEM_LIMIT_BYTES = (V7X_VMEM_BYTES * 3) // 4

ATT_TILE = 256
PROJ_TM = 512
MERGE_TM = 256
ROUTE_TM = 256
PEER_TM = 512
PEER_TE = 512

INT_MIN = -(2 ** 31)

_NT = (((1,), (1,)), ((), ()))


def _alibi_slopes():
    n = A_HEADS + B_HEADS + C_HEADS
    s = [2.0 ** (-8.0 * k / n) for k in range(1, n + 1)]
    return tuple(s[0::3]), tuple(s[1::3]), tuple(s[2::3])


def _dot(a, b):
    return jnp.dot(a, b, preferred_element_type=F32)


def _dot_nt(a, b):
    return lax.dot_general(a, b, _NT, preferred_element_type=F32)


def _split_hi_lo(x):
    hi = x.astype(BF16)
    lo = (x - hi.astype(F32)).astype(BF16)
    return hi, lo


def _params(*sem):
    return pltpu.CompilerParams(dimension_semantics=sem, vmem_limit_bytes=VMEM_LIMIT_BYTES)


def _tile_iotas(n):
    return lax.broadcasted_iota(I32, (n, n), 0), lax.broadcasted_iota(I32, (n, n), 1)


def _proj_kernel(groups, kt, x_ref, *refs):
    n_w = sum(1 if prec == "bf16" else 2 for _, prec in groups)
    w_refs, o_refs = refs[:n_w], refs[n_w:]
    xh, xl = _split_hi_lo(x_ref[...])
    wi = oi = 0
    for layout, prec in groups:
        n_in = 1 if prec == "bf16" else 2
        ws = w_refs[wi:wi + n_in]
        wi += n_in
        n_out = 2 if prec == "hilo" else 1
        outs = o_refs[oi:oi + n_out]
        oi += n_out
        transposed = layout in ("cols", "coltiles")

        def result(pick):
            mm = (lambda w, x: _dot_nt(w, x)) if transposed else (lambda w, x: _dot(x, w))
            wh = pick(ws[0])
            acc = mm(wh, xh)
            if prec != "bf16":
                acc = acc + (mm(pick(ws[1]), xh) + mm(wh, xl))
            return acc

        def emit(store, acc):
            if prec == "bf16":
                store(outs[0], acc.astype(BF16))
            elif prec == "f32":
                store(outs[0], acc)
            else:
                hi, lo = _split_hi_lo(acc)
                store(outs[0], hi)
                store(outs[1], lo)

        if layout in ("rows", "cols"):
            def store_all(ref, val):
                ref[...] = val
            emit(store_all, result(lambda r: r[...]))
        elif layout == "coltiles":
            acc = result(lambda r: r[...])
            for c in range(acc.shape[1] // kt):
                def store_tile(ref, val, c=c):
                    ref[c] = val
                emit(store_tile, acc[:, c * kt:(c + 1) * kt])
        else:
            for h in range(ws[0].shape[0]):
                def store_head(ref, val, h=h):
                    ref[h] = val
                emit(store_head, result(lambda r, h=h: r[h]))


def _project(x2d, groups, tm, kt):
    t_len, kdim = x2d.shape
    ins, in_specs, out_shapes, out_specs = [x2d], [pl.BlockSpec((tm, kdim), lambda i: (i, 0))], [], []
    for w, layout, prec in groups:
        n = w.shape[1]
        if layout in ("cols", "coltiles"):
            wk = w.T
        elif layout == "heads":
            wk = w.reshape(kdim, n // HEAD_DIM, HEAD_DIM).transpose(1, 0, 2)
        else:
            wk = w
        parts = [wk.astype(BF16)] if prec == "bf16" else list(_split_hi_lo(wk))
        for p in parts:
            ins.append(p)
            in_specs.append(pl.BlockSpec(p.shape, lambda i, nd=p.ndim: (0,) * nd))
        odt = F32 if prec == "f32" else BF16
        if layout == "rows":
            shape, spec = (t_len, n), pl.BlockSpec((tm, n), lambda i: (i, 0))
        elif layout == "cols":
            shape, spec = (n, t_len), pl.BlockSpec((n, tm), lambda i: (0, i))
        elif layout == "coltiles":
            shape, spec = (t_len // kt, n, kt), pl.BlockSpec((tm // kt, n, kt), lambda i: (i, 0, 0))
        else:
            nh = n // HEAD_DIM
            shape, spec = (nh, t_len, HEAD_DIM), pl.BlockSpec((nh, tm, HEAD_DIM), lambda i: (0, i, 0))
        for _ in range(2 if prec == "hilo" else 1):
            out_shapes.append(jax.ShapeDtypeStruct(shape, odt))
            out_specs.append(spec)
    return pl.pallas_call(
        functools.partial(_proj_kernel, tuple((l, p) for _, l, p in groups), kt),
        grid=(t_len // tm,),
        in_specs=in_specs, out_specs=out_specs, out_shape=out_shapes,
        compiler_params=_params("arbitrary"),
        name="in_proj",
    )(*ins)


def _q_cols_spec(n, nq):
    return pl.BlockSpec((n, ATT_TILE), lambda b, i: (0, b * nq + i))


def _kv_tiles_spec(n, nk):
    return pl.BlockSpec((nk, n, ATT_TILE), lambda b, i: (b, 0, 0))


def _k_heads_spec(nh, s_len):
    return pl.BlockSpec((nh, s_len, HEAD_DIM), lambda b, i: (0, b, 0))


def _online_softmax_step(s, m_prev, l_prev, acc_ref, h, vt):
    m_next = jnp.maximum(m_prev, jnp.max(s, axis=0, keepdims=True))
    p = jnp.exp(s - m_next)
    alpha = jnp.exp(m_prev - m_next)
    acc_ref[h] = acc_ref[h] * alpha + _dot(vt, p.astype(BF16))
    return m_next, alpha * l_prev + jnp.sum(p, axis=0, keepdims=True)


def _dsa_kernel(qih_ref, qil_ref, wi_ref, k3_ref, qa_ref, ka_ref, vt_ref, o_ref,
                keys_sc, acc_sc, *, n_sel, slopes):
    t = ATT_TILE
    i = pl.program_id(1)
    nch = i + 1
    keyi, qryi = _tile_iotas(t)
    wi = wi_ref[...]
    q3 = []
    for h in range(IDX_HEADS):
        rows = slice(h * IDX_DIM, (h + 1) * IDX_DIM)
        q3.append(jnp.concatenate([qih_ref[rows, :], qil_ref[rows, :], qih_ref[rows, :]], axis=0))

    def score_tile(c, carry):
        k3 = k3_ref[pl.ds(pl.multiple_of(c * t, t), t), :]
        sc = None
        for h in range(IDX_HEADS):
            r = jnp.maximum(_dot(k3, q3[h]), 0.0) * wi[h:h + 1, :]
            sc = r if sc is None else sc + r
        sc = sc * IDX_SCALE
        sc = jnp.where(sc == 0.0, 0.0, sc)
        bits = lax.bitcast_convert_type(sc, I32)
        key = bits ^ ((bits >> 31) & 0x7FFFFFFF)
        valid = ((c - i) * t + keyi) <= qryi
        keys_sc[c] = jnp.where(valid, key, INT_MIN)
        return carry

    lax.fori_loop(0, nch, score_tile, 0)

    def count(pred):
        def body(c, acc):
            ind = jnp.where(pred(keys_sc[c]), 1.0, 0.0)
            return acc + jnp.sum(ind.reshape(t // V7X_SUBLANES, V7X_SUBLANES, t), axis=0)
        acc = lax.fori_loop(0, nch, body, jnp.zeros((V7X_SUBLANES, t), F32))
        return jnp.sum(acc, axis=0, keepdims=True)

    def bit_step(it, tau):
        cand = tau ^ lax.shift_left(jnp.int32(1), 31 - it)
        cnt = count(lambda k: k >= cand)
        return jnp.where(cnt >= n_sel, cand, tau)

    tau = lax.fori_loop(0, 32, bit_step, jnp.full((1, t), INT_MIN, I32))
    need = n_sel - count(lambda k: k > tau)

    acc_sc[...] = jnp.zeros(acc_sc.shape, F32)
    before = jnp.where(qryi < keyi, 1.0, 0.0).astype(BF16)
    qa = qa_ref[...] * QK_SCALE

    def attend_tile(c, carry):
        n_eq_before, ms, ls = carry
        start = pl.multiple_of(c * t, t)
        key = keys_sc[c]
        eq = key == tau
        eqf = jnp.where(eq, 1.0, 0.0)
        taken = (n_eq_before + _dot(before, eqf.astype(BF16))) < need
        bias = jnp.where(key > tau, 0.0, jnp.where(eq, jnp.where(taken, 0.0, NEG_INF), NEG_INF))
        bias = jnp.where(((c - i) * t + keyi) <= qryi, bias, NEG_INF)
        k = ka_ref[pl.ds(start, t), :]
        vt = vt_ref[c]
        dist = ((i - c) * t + qryi - keyi).astype(F32)
        ms_next, ls_next = [], []
        for h in range(A_HEADS):
            s = _dot(k, qa[h * HEAD_DIM:(h + 1) * HEAD_DIM]) - slopes[h] * dist
            s = jnp.maximum(s + bias, NEG_INF)
            m, l = _online_softmax_step(s, ms[h], ls[h], acc_sc, h, vt)
            ms_next.append(m)
            ls_next.append(l)
        return n_eq_before + jnp.sum(eqf, axis=0, keepdims=True), tuple(ms_next), tuple(ls_next)

    row = lambda v: tuple(jnp.full((1, t), v, F32) for _ in range(A_HEADS))
    _, _, ls = lax.fori_loop(0, nch, attend_tile, (jnp.zeros((1, t), F32), row(NEG_INF), row(0.0)))
    for h in range(A_HEADS):
        o_ref[h * HEAD_DIM:(h + 1) * HEAD_DIM, :] = acc_sc[h] / ls[h]


def _dsa(qih, qil, wi, k3, qa, ka, vt, bn, s_len, slopes):
    nq = s_len // ATT_TILE
    n_sel = min(DSA_TOPK_MAX, s_len // 4)
    return pl.pallas_call(
        functools.partial(_dsa_kernel, n_sel=float(n_sel), slopes=slopes),
        grid=(bn, nq),
        in_specs=[_q_cols_spec(IDX_HEADS * IDX_DIM, nq), _q_cols_spec(IDX_HEADS * IDX_DIM, nq),
                  _q_cols_spec(IDX_HEADS, nq),
                  pl.BlockSpec((s_len, 3 * IDX_DIM), lambda b, i: (b, 0)),
                  _q_cols_spec(A_HEADS * HEAD_DIM, nq),
                  pl.BlockSpec((s_len, HEAD_DIM), lambda b, i: (b, 0)),
                  _kv_tiles_spec(HEAD_DIM, nq)],
        out_specs=_q_cols_spec(A_HEADS * HEAD_DIM, nq),
        out_shape=jax.ShapeDtypeStruct((A_HEADS * HEAD_DIM, bn * s_len), F32),
        scratch_shapes=[pltpu.VMEM((nq, ATT_TILE, ATT_TILE), I32),
                        pltpu.VMEM((A_HEADS, HEAD_DIM, ATT_TILE), F32)],
        compiler_params=_params("arbitrary", "arbitrary"),
        name="dsa",
    )(qih, qil, wi, k3, qa, ka, vt)


def _swa_kernel(sink_ref, q_ref, kp_ref, kc_ref, vp_ref, vc_ref, o_ref, *, slopes):
    w = WINDOW
    i = pl.program_id(1)
    grp = B_HEADS // B_KV_HEADS
    row = lax.broadcasted_iota(I32, (w, w), 0)
    col = lax.broadcasted_iota(I32, (w, w), 1)
    valid_p = jnp.logical_and(col > row, i > 0)
    valid_c = col <= row
    dist_p = (w + row - col).astype(F32)
    dist_c = (row - col).astype(F32)
    for kh in range(B_KV_HEADS):
        q2 = q_ref[kh * grp:(kh + 1) * grp].reshape(grp * w, HEAD_DIM) * QK_SCALE
        sp = _dot_nt(q2, kp_ref[kh])
        sc = _dot_nt(q2, kc_ref[kh])
        for g in range(grp):
            h = kh * grp + g
            lp = jnp.where(valid_p, sp[g * w:(g + 1) * w] - slopes[h] * dist_p, NEG_INF)
            lc = jnp.where(valid_c, sc[g * w:(g + 1) * w] - slopes[h] * dist_c, NEG_INF)
            sink = sink_ref[h]
            m = jnp.maximum(jnp.maximum(jnp.max(lp, axis=1, keepdims=True),
                                        jnp.max(lc, axis=1, keepdims=True)), sink)
            pp = jnp.exp(lp - m)
            pc = jnp.exp(lc - m)
            den = (jnp.sum(pp, axis=1, keepdims=True) + jnp.sum(pc, axis=1, keepdims=True)
                   + jnp.exp(sink - m))
            o = _dot(pp.astype(BF16), vp_ref[kh]) + _dot(pc.astype(BF16), vc_ref[kh])
            o_ref[h * HEAD_DIM:(h + 1) * HEAD_DIM, :] = (o / den).T


def _swa(sinks, q, k, v, bn, s_len, slopes):
    nb = s_len // WINDOW
    kv = lambda imap: pl.BlockSpec((B_KV_HEADS, WINDOW, HEAD_DIM), imap)
    cur = lambda b, i: (0, b * nb + i, 0)
    prev = lambda b, i: (0, b * nb + jnp.maximum(i - 1, 0), 0)
    return pl.pallas_call(
        functools.partial(_swa_kernel, slopes=slopes),
        grid=(bn, nb),
        in_specs=[pl.BlockSpec(memory_space=pltpu.SMEM),
                  pl.BlockSpec((B_HEADS, WINDOW, HEAD_DIM), cur), kv(prev), kv(cur), kv(prev), kv(cur)],
        out_specs=pl.BlockSpec((B_HEADS * HEAD_DIM, WINDOW), lambda b, i: (0, b * nb + i)),
        out_shape=jax.ShapeDtypeStruct((B_HEADS * HEAD_DIM, bn * s_len), F32),
        compiler_params=_params("arbitrary", "arbitrary"),
        name="swa",
    )(sinks, q, k, k, v, v)


def _moba_kernel(qh_ref, ql_ref, k_ref, vt_ref, o_ref, kmh_sc, kml_sc, sel_sc, acc_sc,
                 *, nblk, n_top, slopes):
    t = MOBA_BLOCK
    i = pl.program_id(1)
    keyi, qryi = _tile_iotas(t)

    @pl.when(i == 0)
    def _():
        for h in range(C_HEADS):
            km = jnp.sum(k_ref[h].reshape(nblk, t, HEAD_DIM), axis=1) * (1.0 / t)
            kmh, kml = _split_hi_lo(km)
            kmh_sc[h] = kmh
            kml_sc[h] = kml

    blk = lax.broadcasted_iota(I32, (nblk, t), 0)
    for h in range(C_HEADS):
        rows = slice(h * HEAD_DIM, (h + 1) * HEAD_DIM)
        qh, ql = qh_ref[rows, :], ql_ref[rows, :]
        gate = _dot(kmh_sc[h], qh) + (_dot(kmh_sc[h], ql) + _dot(kml_sc[h], qh))
        rank = jnp.zeros((nblk, t), F32)
        for j in range(nblk):
            gj = gate[j:j + 1, :]
            beats = (gj > gate) | ((gj == gate) & (j < blk))
            rank = rank + jnp.where(jnp.logical_and(beats, j < i), 1.0, 0.0)
        sel_sc[h] = jnp.where((rank < n_top) & (blk < i), 1.0, 0.0)

    acc_sc[...] = jnp.zeros(acc_sc.shape, F32)

    def attend(j, ms, ls, mask_of_head):
        start = pl.multiple_of(j * t, t)
        vt = vt_ref[j]
        dist = ((i - j) * t + qryi - keyi).astype(F32)
        ms_next, ls_next = [], []
        for h in range(C_HEADS):
            rows = slice(h * HEAD_DIM, (h + 1) * HEAD_DIM)
            k = k_ref[h, pl.ds(start, t), :].astype(BF16)
            s = _dot(k, qh_ref[rows, :] * QK_SCALE) - slopes[h] * dist
            s = jnp.where(mask_of_head(h), s, NEG_INF)
            m, l = _online_softmax_step(s, ms[h], ls[h], acc_sc, h, vt[rows])
            ms_next.append(m)
            ls_next.append(l)
        return tuple(ms_next), tuple(ls_next)

    def past_block(j, carry):
        return attend(j, *carry, lambda h: sel_sc[h, pl.ds(j, 1), :] > 0.5)

    row = lambda v: tuple(jnp.full((1, t), v, F32) for _ in range(C_HEADS))
    ms, ls = lax.fori_loop(0, i, past_block, (row(NEG_INF), row(0.0)))
    _, ls = attend(i, ms, ls, lambda h: keyi <= qryi)
    for h in range(C_HEADS):
        o_ref[h * HEAD_DIM:(h + 1) * HEAD_DIM, :] = acc_sc[h] / ls[h]


def _moba(qh, ql, k, vt, bn, s_len, slopes):
    nblk = s_len // MOBA_BLOCK
    n_top = min(MOBA_TOPK, nblk - 1)
    width = C_HEADS * HEAD_DIM
    return pl.pallas_call(
        functools.partial(_moba_kernel, nblk=nblk, n_top=float(n_top), slopes=slopes),
        grid=(bn, nblk),
        in_specs=[_q_cols_spec(width, nblk), _q_cols_spec(width, nblk),
                  _k_heads_spec(C_HEADS, s_len), _kv_tiles_spec(width, nblk)],
        out_specs=_q_cols_spec(width, nblk),
        out_shape=jax.ShapeDtypeStruct((width, bn * s_len), F32),
        scratch_shapes=[pltpu.VMEM((C_HEADS, nblk, HEAD_DIM), BF16),
                        pltpu.VMEM((C_HEADS, nblk, HEAD_DIM), BF16),
                        pltpu.VMEM((C_HEADS, nblk, MOBA_BLOCK), F32),
                        pltpu.VMEM((C_HEADS, HEAD_DIM, MOBA_BLOCK), F32)],
        compiler_params=_params("arbitrary", "arbitrary"),
        name="moba",
    )(qh, ql, k, vt)


def _stick_kernel(q_ref, k_ref, vt_ref, o_ref, acc_sc):
    t = ATT_TILE
    i = pl.program_id(1)
    keyi, qryi = _tile_iotas(t)
    after = jnp.where(qryi > keyi, 1.0, 0.0).astype(BF16)
    q = q_ref[...] * QK_SCALE
    acc_sc[...] = jnp.zeros(acc_sc.shape, F32)

    def tile(it, later):
        j = i - it
        start = pl.multiple_of(j * t, t)
        strict = ((j - i) * t + keyi) < qryi
        vt = vt_ref[j]
        later_next = []
        for h in range(D_HEADS):
            rows = slice(h * HEAD_DIM, (h + 1) * HEAD_DIM)
            z = _dot(k_ref[h, pl.ds(start, t), :], q[rows])
            sp = jnp.log1p(jnp.exp(-jnp.abs(z)))
            log_keep = jnp.where(strict, -(jnp.maximum(z, 0.0) + sp), 0.0)
            log_beta = jnp.minimum(z, 0.0) - sp
            lk_hi, lk_lo = _split_hi_lo(log_keep)
            log_after = later[h] + (_dot(after, lk_hi) + _dot(after, lk_lo))
            a = jnp.where(strict, jnp.exp(log_beta + log_after), 0.0)
            acc_sc[h] += _dot(vt[rows], a.astype(BF16))
            later_next.append(later[h] + jnp.sum(log_keep, axis=0, keepdims=True))
        return tuple(later_next)

    lax.fori_loop(0, i + 1, tile, tuple(jnp.zeros((1, t), F32) for _ in range(D_HEADS)))
    for h in range(D_HEADS):
        o_ref[h * HEAD_DIM:(h + 1) * HEAD_DIM, :] = acc_sc[h]


def _stick(q, k, vt, bn, s_len):
    nq = s_len // ATT_TILE
    width = D_HEADS * HEAD_DIM
    return pl.pallas_call(
        _stick_kernel,
        grid=(bn, nq),
        in_specs=[_q_cols_spec(width, nq), _k_heads_spec(D_HEADS, s_len), _kv_tiles_spec(width, nq)],
        out_specs=_q_cols_spec(width, nq),
        out_shape=jax.ShapeDtypeStruct((width, bn * s_len), F32),
        scratch_shapes=[pltpu.VMEM((D_HEADS, HEAD_DIM, ATT_TILE), F32)],
        compiler_params=_params("arbitrary", "arbitrary"),
        name="stick",
    )(q, k, vt)


def _layer_norm(r, g, b):
    mu = jnp.mean(r, axis=-1, keepdims=True)
    d = r - mu
    var = jnp.mean(d * d, axis=-1, keepdims=True)
    return d * lax.rsqrt(var + LN_EPS) * g + b


def _merge_kernel(x_ref, ya_ref, yb_ref, yc_ref, yd_ref, wg_ref, wb_ref, wo_ref, g_ref, b_ref, h_ref):
    x = x_ref[...]
    xb = x.astype(BF16)
    merged = None
    for n, yt_ref in enumerate((ya_ref, yb_ref, yc_ref, yd_ref)):
        gate = jax.nn.sigmoid(_dot(xb, wg_ref[:, n * D_MODEL:(n + 1) * D_MODEL]))
        term = gate * _dot(yt_ref[...].T.astype(BF16), wb_ref[n])
        merged = term if merged is None else merged + term
    mix = _dot(merged.astype(BF16), wo_ref[...])
    h_ref[...] = _layer_norm(DEEPNORM_ALPHA * x + mix, g_ref[...], b_ref[...])


def _merge(x2d, yts, w_gate, w_branch, w_out, g, b, tm):
    t_len = x2d.shape[0]
    row = pl.BlockSpec((tm, D_MODEL), lambda i: (i, 0))
    col = pl.BlockSpec((BRANCH_WIDTH, tm), lambda i: (0, i))
    full = lambda shape: pl.BlockSpec(shape, lambda i: (0,) * len(shape))
    return pl.pallas_call(
        _merge_kernel,
        grid=(t_len // tm,),
        in_specs=[row] + [col] * N_BRANCH
                 + [full(w_gate.shape), full(w_branch.shape), full(w_out.shape),
                    full((1, D_MODEL)), full((1, D_MODEL))],
        out_specs=row,
        out_shape=jax.ShapeDtypeStruct((t_len, D_MODEL), F32),
        compiler_params=_params("arbitrary"),
        name="merge_ln",
    )(x2d, *yts, w_gate, w_branch, w_out, g.reshape(1, D_MODEL), b.reshape(1, D_MODEL))


def _top16_rows(s, vals_sc):
    n, tm = s.shape
    rows = lax.broadcasted_iota(I32, (n, tm), 0).astype(F32)
    rank = jnp.full((n, tm), float(PEER_TOPK), F32)
    work = s
    for k in range(PEER_TOPK):
        m = jnp.max(work, axis=0, keepdims=True)
        first = jnp.min(jnp.where(work == m, rows, float(n)), axis=0, keepdims=True)
        hit = rows == first
        rank = jnp.where(hit, float(k), rank)
        work = jnp.where(hit, -jnp.inf, work)
        vals_sc[k:k + 1, :] = m
    return rank


def _peer_route_kernel(h_ref, wqh_ref, wql_ref, wsh_ref, wsl_ref,
                       rank2_ref, r_ref, e2_ref, e1n_ref, a_sc, b_sc, c_sc, top_sc):
    tm = h_ref.shape[0]
    hh, hl = _split_hi_lo(h_ref[...])
    qt = _dot_nt(wqh_ref[...], hh) + (_dot_nt(wql_ref[...], hh) + _dot_nt(wqh_ref[...], hl))
    for h in range(PEER_HEADS):
        qh, ql = _split_hi_lo(qt[h * PEER_QDIM:(h + 1) * PEER_QDIM, :])
        st = _dot(wsh_ref[h], qh) + (_dot(wsl_ref[h], qh) + _dot(wsh_ref[h], ql))
        s1, s2 = st[:N_KEYS], st[N_KEYS:]
        rank1 = _top16_rows(s1, a_sc)
        rank2 = _top16_rows(s2, b_sc)
        a16, b16 = a_sc[...], b_sc[...]
        c_sc[0:16, :] = a16[0:1] + b16
        for ia in range(1, 8):
            c_sc[8 + 8 * ia:16 + 8 * ia, :] = a16[ia:ia + 1] + b16[0:8]
        c_sc[72:80, :] = a16[8:16] + b16[0:1]
        _top16_rows(c_sc[...], top_sc)
        top = top_sc[...]
        tau = top[PEER_TOPK - 1:PEER_TOPK]
        z = jnp.sum(jnp.exp(top - top[0:1]), axis=0, keepdims=True)
        r = jnp.zeros((N_KEYS, tm), F32)
        for j in range(PEER_TOPK):
            r = r + jnp.where((s1 + b16[j:j + 1]) >= tau, 1.0, 0.0)
        r_ref[h] = jnp.where(rank1 < PEER_TOPK, r, 0.0)
        rank2_ref[h] = rank2.astype(BF16)
        e2_ref[h] = jnp.exp(s2 - b16[0:1]).astype(BF16)
        e1n_ref[h] = jnp.exp(s1 - a16[0:1]) / z


def _peer_route(h2d, wq_t, wsub_t, tm):
    t_len = h2d.shape[0]
    wqh, wql = _split_hi_lo(wq_t)
    wsh, wsl = _split_hi_lo(wsub_t)
    full = lambda shape: pl.BlockSpec(shape, lambda i: (0,) * len(shape))
    ospec = pl.BlockSpec((PEER_HEADS, N_KEYS, tm), lambda i: (0, 0, i))
    oshape = lambda dt: jax.ShapeDtypeStruct((PEER_HEADS, N_KEYS, t_len), dt)
    return pl.pallas_call(
        _peer_route_kernel,
        grid=(t_len // tm,),
        in_specs=[pl.BlockSpec((tm, D_MODEL), lambda i: (i, 0)),
                  full(wqh.shape), full(wql.shape), full(wsh.shape), full(wsl.shape)],
        out_specs=[ospec] * 4,
        out_shape=[oshape(BF16), oshape(F32), oshape(BF16), oshape(F32)],
        scratch_shapes=[pltpu.VMEM((PEER_TOPK, tm), F32), pltpu.VMEM((PEER_TOPK, tm), F32),
                        pltpu.VMEM((80, tm), F32), pltpu.VMEM((PEER_TOPK, tm), F32)],
        compiler_params=_params("arbitrary"),
        name="peer_route",
    )(h2d, wqh, wql, wsh, wsl)


def _packed_rows(row):
    return jnp.broadcast_to(row, (BF16_ROWS, row.shape[1])).astype(BF16)


def _peer_dense_kernel(h_ref, u_ref, vt_ref, rank2_ref, r_ref, e2_ref, e1n_ref, g_ref, b_ref,
                       o_ref, xb_sc, ht_sc, p_sc, acc_sc, *, te):
    j = pl.program_id(1)
    tm = h_ref.shape[0]
    rows_per_step = te // N_KEYS

    @pl.when(j == 0)
    def _():
        acc_sc[...] = jnp.zeros(acc_sc.shape, F32)
        xb_sc[...] = h_ref[...].astype(BF16)

    ht_sc[...] = _dot_nt(u_ref[...], xb_sc[...])
    for a in range(rows_per_step):
        i1 = j * rows_per_step + a
        rs = slice(a * N_KEYS, (a + 1) * N_KEYS)
        r_rows = [_packed_rows(r_ref[h, pl.ds(i1, 1), :]) for h in range(PEER_HEADS)]
        w_rows = [_packed_rows(e1n_ref[h, pl.ds(i1, 1), :]) for h in range(PEER_HEADS)]
        for c in range(tm // V7X_LANES):
            cs = slice(c * V7X_LANES, (c + 1) * V7X_LANES)
            gsum = None
            for h in range(PEER_HEADS):
                r_tile = jnp.tile(r_rows[h][:, cs], (N_KEYS // BF16_ROWS, 1))
                w_tile = jnp.tile(w_rows[h][:, cs], (N_KEYS // BF16_ROWS, 1))
                term = jnp.where(rank2_ref[h, :, cs] < r_tile, e2_ref[h, :, cs] * w_tile, 0.0)
                gsum = term if gsum is None else gsum + term
            ht = ht_sc[rs, cs]
            act = 0.5 * ht * (1.0 + lax.erf(ht * float(np.sqrt(0.5))))
            p_sc[rs, cs] = gsum * act.astype(BF16)
    acc_sc[...] += _dot(vt_ref[...], p_sc[...])

    @pl.when(j == pl.num_programs(1) - 1)
    def _():
        r = DEEPNORM_ALPHA * h_ref[...] + acc_sc[...].T
        o_ref[...] = _layer_norm(r, g_ref[...], b_ref[...])


def _peer_dense(h2d, u_bf, vt_bf, rank2, r, e2, e1n, g, b, tm, te):
    t_len = h2d.shape[0]
    rspec = pl.BlockSpec((PEER_HEADS, N_KEYS, tm), lambda i, j: (0, 0, i))
    vec = pl.BlockSpec((1, D_MODEL), lambda i, j: (0, 0))
    return pl.pallas_call(
        functools.partial(_peer_dense_kernel, te=te),
        grid=(t_len // tm, N_EXPERTS // te),
        in_specs=[pl.BlockSpec((tm, D_MODEL), lambda i, j: (i, 0)),
                  pl.BlockSpec((te, D_MODEL), lambda i, j: (j, 0)),
                  pl.BlockSpec((D_MODEL, te), lambda i, j: (0, j)),
                  rspec, rspec, rspec, rspec, vec, vec],
        out_specs=pl.BlockSpec((tm, D_MODEL), lambda i, j: (i, 0)),
        out_shape=jax.ShapeDtypeStruct((t_len, D_MODEL), F32),
        scratch_shapes=[pltpu.VMEM((tm, D_MODEL), BF16), pltpu.VMEM((te, tm), F32),
                        pltpu.VMEM((te, tm), BF16), pltpu.VMEM((D_MODEL, tm), F32)],
        compiler_params=_params("arbitrary", "arbitrary"),
        name="peer_dense",
    )(h2d, u_bf, vt_bf, rank2, r, e2, e1n, g.reshape(1, D_MODEL), b.reshape(1, D_MODEL))


def _sub_key_blocks(subkeys):
    hn, _, nk, d = subkeys.shape
    z = jnp.zeros((hn, nk, d), subkeys.dtype)
    top = jnp.concatenate([subkeys[:, 0], z], axis=-1)
    bot = jnp.concatenate([z, subkeys[:, 1]], axis=-1)
    return jnp.concatenate([top, bot], axis=1)


_PROJ_GROUPS = (
    ("cols", "bf16"), ("rows", "bf16"), ("coltiles", "bf16"),
    ("cols", "hilo"), ("rows", "hilo"), ("cols", "f32"),
    ("heads", "bf16"), ("heads", "bf16"), ("heads", "bf16"),
    ("cols", "hilo"), ("heads", "f32"), ("coltiles", "bf16"),
    ("cols", "bf16"), ("heads", "bf16"), ("coltiles", "bf16"),
)


def _token_mixer(h2d, bn, s_len, w_in, w_branch, w_out, sinks, ln_g, ln_b, slopes):
    offs = np.concatenate([[0], np.cumsum(IN_WIDTHS)])
    cols = [w_in[:, offs[k]:offs[k + 1]] for k in range(len(IN_WIDTHS))]
    groups = [(cols[k], layout, prec) for k, (layout, prec) in enumerate(_PROJ_GROUPS)]
    (qa, ka, va, qi_h, qi_l, ki_h, ki_l, wi, qb, kb, vb, qc_h, qc_l, kc, vc, qd, kd, vd) = _project(
        h2d, groups, tm=PROJ_TM, kt=ATT_TILE)
    k3 = jnp.concatenate([ki_h, ki_h, ki_l], axis=-1)
    yts = [_dsa(qi_h, qi_l, wi, k3, qa, ka, va, bn, s_len, slopes[0]),
           _swa(sinks, qb, kb, vb, bn, s_len, slopes[1]),
           _moba(qc_h, qc_l, kc, vc, bn, s_len, slopes[2]),
           _stick(qd, kd, vd, bn, s_len)]
    return _merge(h2d, yts, cols[15].astype(BF16), w_branch.astype(BF16), w_out.astype(BF16),
                  ln_g, ln_b, tm=MERGE_TM)


def _peer(h2d, wq, subkeys, u, v, ln_g, ln_b):
    rank2, r, e2, e1n = _peer_route(h2d, wq.T, _sub_key_blocks(subkeys), tm=ROUTE_TM)
    return _peer_dense(h2d, u.astype(BF16), v.T.astype(BF16), rank2, r, e2, e1n, ln_g, ln_b,
                       tm=PEER_TM, te=PEER_TE)


def kernel(x, w_in, w_branch, w_out, attn_sinks, ln1_g, ln1_b, peer_wq, peer_subkeys,
           peer_u, peer_v, ln2_g, ln2_b):
    bn, s_len, d = x.shape
    slopes = _alibi_slopes()
    h = x.reshape(bn * s_len, d)
    for l in range(DEPTH):
        h = _token_mixer(h, bn, s_len, w_in[l], w_branch[l], w_out[l], attn_sinks[l],
                         ln1_g[l], ln1_b[l], slopes)
        h = _peer(h, peer_wq[l], peer_subkeys[l], peer_u[l], peer_v[l], ln2_g[l], ln2_b[l])
    return h.reshape(bn, s_len, d)
```

```python
import functools

import jax
import jax.numpy as jnp
import numpy as np
from jax import lax
from jax.experimental import pallas as pl
from jax.experimental.pallas import tpu as pltpu

F32 = jnp.float32
BF16 = jnp.bfloat16
I32 = jnp.int32

D_MODEL = 1024
DEPTH = 2
HEAD_DIM = 64
A_HEADS = 4
IDX_HEADS = 8
IDX_DIM = 32
DSA_TOPK_MAX = 256
B_HEADS = 4
B_KV_HEADS = 2
WINDOW = 128
C_HEADS = 4
MOBA_BLOCK = 256
MOBA_TOPK = 3
D_HEADS = 4
N_BRANCH = 4
BRANCH_WIDTH = 4 * HEAD_DIM
PEER_HEADS = 8
N_KEYS = 128
N_EXPERTS = N_KEYS * N_KEYS
PEER_QDIM = 128
PEER_TOPK = 16
LN_EPS = 1e-5
NEG_INF = -1e30
DEEPNORM_ALPHA = (2 * DEPTH) ** 0.25
IDX_SCALE = (IDX_DIM * IDX_HEADS) ** -0.5
QK_SCALE = HEAD_DIM ** -0.5

IN_WIDTHS = (
    A_HEADS * HEAD_DIM, HEAD_DIM, HEAD_DIM,
    IDX_HEADS * IDX_DIM, IDX_DIM, IDX_HEADS,
    B_HEADS * HEAD_DIM, B_KV_HEADS * HEAD_DIM, B_KV_HEADS * HEAD_DIM,
    C_HEADS * HEAD_DIM, C_HEADS * HEAD_DIM, C_HEADS * HEAD_DIM,
    D_HEADS * HEAD_DIM, D_HEADS * HEAD_DIM, D_HEADS * HEAD_DIM,
    N_BRANCH * D_MODEL,
)

V7X_LANES = 128
V7X_SUBLANES = 8
BF16_ROWS = 2 * V7X_SUBLANES
BF16_MIN_LANES = 2 * V7X_LANES
V7X_VMEM_BYTES = 64 * 1024 * 1024
VMEM_LIMIT_BYTES = (V7X_VMEM_BYTES * 3) // 4

ATT_TILE = 256
PROJ_TM = 512
MERGE_TM = 256
ROUTE_TM = 256
PEER_TM = 512
PEER_TE = 1024

INT_MIN = -(2 ** 31)

_NT = (((1,), (1,)), ((), ()))


def _alibi_slopes():
    n = A_HEADS + B_HEADS + C_HEADS
    s = [2.0 ** (-8.0 * k / n) for k in range(1, n + 1)]
    return tuple(s[0::3]), tuple(s[1::3]), tuple(s[2::3])


def _dot(a, b):
    return jnp.dot(a, b, preferred_element_type=F32)


def _dot_nt(a, b):
    return lax.dot_general(a, b, _NT, preferred_element_type=F32)


def _split_hi_lo(x):
    hi = x.astype(BF16)
    lo = (x - hi.astype(F32)).astype(BF16)
    return hi, lo


def _params(*sem):
    return pltpu.CompilerParams(dimension_semantics=sem, vmem_limit_bytes=VMEM_LIMIT_BYTES)


def _tile_iotas(n):
    return lax.broadcasted_iota(I32, (n, n), 0), lax.broadcasted_iota(I32, (n, n), 1)


def _proj_kernel(groups, kt, x_ref, *refs):
    n_w = sum(1 if prec == "bf16" else 2 for _, prec in groups)
    w_refs, o_refs = refs[:n_w], refs[n_w:]
    xh, xl = _split_hi_lo(x_ref[...])
    wi = oi = 0
    for layout, prec in groups:
        n_in = 1 if prec == "bf16" else 2
        ws = w_refs[wi:wi + n_in]
        wi += n_in
        n_out = 2 if prec == "hilo" else 1
        outs = o_refs[oi:oi + n_out]
        oi += n_out
        transposed = layout in ("cols", "coltiles")

        def result(pick):
            mm = (lambda w, x: _dot_nt(w, x)) if transposed else (lambda w, x: _dot(x, w))
            wh = pick(ws[0])
            acc = mm(wh, xh)
            if prec != "bf16":
                acc = acc + (mm(pick(ws[1]), xh) + mm(wh, xl))
            return acc

        def emit(store, acc):
            if prec == "bf16":
                store(outs[0], acc.astype(BF16))
            elif prec == "f32":
                store(outs[0], acc)
            else:
                hi, lo = _split_hi_lo(acc)
                store(outs[0], hi)
                store(outs[1], lo)

        if layout in ("rows", "cols"):
            def store_all(ref, val):
                ref[...] = val
            emit(store_all, result(lambda r: r[...]))
        elif layout == "coltiles":
            acc = result(lambda r: r[...])
            for c in range(acc.shape[1] // kt):
                def store_tile(ref, val, c=c):
                    ref[c] = val
                emit(store_tile, acc[:, c * kt:(c + 1) * kt])
        else:
            for h in range(ws[0].shape[0]):
                def store_head(ref, val, h=h):
                    ref[h] = val
                emit(store_head, result(lambda r, h=h: r[h]))


def _project(x2d, groups, tm, kt):
    t_len, kdim = x2d.shape
    ins, in_specs, out_shapes, out_specs = [x2d], [pl.BlockSpec((tm, kdim), lambda i: (i, 0))], [], []
    for w, layout, prec in groups:
        n = w.shape[1]
        if layout in ("cols", "coltiles"):
            wk = w.T
        elif layout == "heads":
            wk = w.reshape(kdim, n // HEAD_DIM, HEAD_DIM).transpose(1, 0, 2)
        else:
            wk = w
        parts = [wk.astype(BF16)] if prec == "bf16" else list(_split_hi_lo(wk))
        for p in parts:
            ins.append(p)
            in_specs.append(pl.BlockSpec(p.shape, lambda i, nd=p.ndim: (0,) * nd))
        odt = F32 if prec == "f32" else BF16
        if layout == "rows":
            shape, spec = (t_len, n), pl.BlockSpec((tm, n), lambda i: (i, 0))
        elif layout == "cols":
            shape, spec = (n, t_len), pl.BlockSpec((n, tm), lambda i: (0, i))
        elif layout == "coltiles":
            shape, spec = (t_len // kt, n, kt), pl.BlockSpec((tm // kt, n, kt), lambda i: (i, 0, 0))
        else:
            nh = n // HEAD_DIM
            shape, spec = (nh, t_len, HEAD_DIM), pl.BlockSpec((nh, tm, HEAD_DIM), lambda i: (0, i, 0))
        for _ in range(2 if prec == "hilo" else 1):
            out_shapes.append(jax.ShapeDtypeStruct(shape, odt))
            out_specs.append(spec)
    return pl.pallas_call(
        functools.partial(_proj_kernel, tuple((l, p) for _, l, p in groups), kt),
        grid=(t_len // tm,),
        in_specs=in_specs, out_specs=out_specs, out_shape=out_shapes,
        compiler_params=_params("arbitrary"),
        name="in_proj",
    )(*ins)


def _q_cols_spec(n, nq):
    return pl.BlockSpec((n, ATT_TILE), lambda b, i: (0, b * nq + i))


def _kv_tiles_spec(n, nk):
    return pl.BlockSpec((nk, n, ATT_TILE), lambda b, i: (b, 0, 0))


def _k_heads_spec(nh, s_len):
    return pl.BlockSpec((nh, s_len, HEAD_DIM), lambda b, i: (0, b, 0))


def _online_softmax_step(ss, ms, ls, acc_ref, vts):
    m_next = [jnp.maximum(m, jnp.max(s, axis=0, keepdims=True)) for s, m in zip(ss, ms)]
    ps = [jnp.exp(s - m) for s, m in zip(ss, m_next)]
    alphas = [jnp.exp(m0 - m1) for m0, m1 in zip(ms, m_next)]
    pvs = [_dot(vt, p.astype(BF16)) for vt, p in zip(vts, ps)]
    l_next = [a * l + jnp.sum(p, axis=0, keepdims=True) for a, l, p in zip(alphas, ls, ps)]
    for h in range(len(ss)):
        acc_ref[h] = acc_ref[h] * alphas[h] + pvs[h]
    return tuple(m_next), tuple(l_next)


def _dsa_kernel(qih_ref, qil_ref, wi_ref, k3_ref, qa_ref, ka_ref, vt_ref, o_ref,
                keys_sc, acc_sc, *, n_sel, slopes):
    t = ATT_TILE
    i = pl.program_id(1)
    nch = i + 1
    keyi, qryi = _tile_iotas(t)
    wi = wi_ref[...]
    q3 = []
    for h in range(IDX_HEADS):
        rows = slice(h * IDX_DIM, (h + 1) * IDX_DIM)
        q3.append(jnp.concatenate([qih_ref[rows, :], qil_ref[rows, :], qih_ref[rows, :]], axis=0))

    def score_tile(c, carry):
        k3 = k3_ref[pl.ds(pl.multiple_of(c * t, t), t), :]
        sc = None
        for h in range(IDX_HEADS):
            r = jnp.maximum(_dot(k3, q3[h]), 0.0) * wi[h:h + 1, :]
            sc = r if sc is None else sc + r
        sc = sc * IDX_SCALE
        sc = jnp.where(sc == 0.0, 0.0, sc)
        bits = lax.bitcast_convert_type(sc, I32)
        key = bits ^ ((bits >> 31) & 0x7FFFFFFF)
        valid = ((c - i) * t + keyi) <= qryi
        keys_sc[c] = jnp.where(valid, key, INT_MIN)
        return carry

    lax.fori_loop(0, nch, score_tile, 0)

    def count(pred):
        def body(c, acc):
            ind = jnp.where(pred(keys_sc[c]), 1.0, 0.0)
            return acc + jnp.sum(ind.reshape(t // V7X_SUBLANES, V7X_SUBLANES, t), axis=0)
        acc = lax.fori_loop(0, nch, body, jnp.zeros((V7X_SUBLANES, t), F32))
        return jnp.sum(acc, axis=0, keepdims=True)

    def bit_step(it, tau):
        cand = tau ^ lax.shift_left(jnp.int32(1), 31 - it)
        cnt = count(lambda k: k >= cand)
        return jnp.where(cnt >= n_sel, cand, tau)

    tau = lax.fori_loop(0, 32, bit_step, jnp.full((1, t), INT_MIN, I32))
    need = n_sel - count(lambda k: k > tau)

    acc_sc[...] = jnp.zeros(acc_sc.shape, F32)
    before = jnp.where(qryi < keyi, 1.0, 0.0).astype(BF16)
    qa = qa_ref[...] * QK_SCALE

    def attend_tile(c, carry):
        n_eq_before, ms, ls = carry
        start = pl.multiple_of(c * t, t)
        key = keys_sc[c]
        eq = key == tau
        eqf = jnp.where(eq, 1.0, 0.0)
        taken = (n_eq_before + _dot(before, eqf.astype(BF16))) < need
        bias = jnp.where(key > tau, 0.0, jnp.where(eq, jnp.where(taken, 0.0, NEG_INF), NEG_INF))
        bias = jnp.where(((c - i) * t + keyi) <= qryi, bias, NEG_INF)
        k = ka_ref[pl.ds(start, t), :]
        vt = vt_ref[c]
        dist = ((i - c) * t + qryi - keyi).astype(F32)
        ss = [_dot(k, qa[h * HEAD_DIM:(h + 1) * HEAD_DIM]) - slopes[h] * dist + bias for h in range(A_HEADS)]
        ms, ls = _online_softmax_step(ss, ms, ls, acc_sc, [vt] * A_HEADS)
        return n_eq_before + jnp.sum(eqf, axis=0, keepdims=True), ms, ls

    row = lambda v: tuple(jnp.full((1, t), v, F32) for _ in range(A_HEADS))
    _, _, ls = lax.fori_loop(0, nch, attend_tile, (jnp.zeros((1, t), F32), row(NEG_INF), row(0.0)))
    for h in range(A_HEADS):
        o_ref[h * HEAD_DIM:(h + 1) * HEAD_DIM, :] = acc_sc[h] / ls[h]


def _dsa(qih, qil, wi, k3, qa, ka, vt, bn, s_len, slopes):
    nq = s_len // ATT_TILE
    n_sel = min(DSA_TOPK_MAX, s_len // 4)
    return pl.pallas_call(
        functools.partial(_dsa_kernel, n_sel=float(n_sel), slopes=slopes),
        grid=(bn, nq),
        in_specs=[_q_cols_spec(IDX_HEADS * IDX_DIM, nq), _q_cols_spec(IDX_HEADS * IDX_DIM, nq),
                  _q_cols_spec(IDX_HEADS, nq),
                  pl.BlockSpec((s_len, 3 * IDX_DIM), lambda b, i: (b, 0)),
                  _q_cols_spec(A_HEADS * HEAD_DIM, nq),
                  pl.BlockSpec((s_len, HEAD_DIM), lambda b, i: (b, 0)),
                  _kv_tiles_spec(HEAD_DIM, nq)],
        out_specs=_q_cols_spec(A_HEADS * HEAD_DIM, nq),
        out_shape=jax.ShapeDtypeStruct((A_HEADS * HEAD_DIM, bn * s_len), F32),
        scratch_shapes=[pltpu.VMEM((nq, ATT_TILE, ATT_TILE), I32),
                        pltpu.VMEM((A_HEADS, HEAD_DIM, ATT_TILE), F32)],
        compiler_params=_params("arbitrary", "arbitrary"),
        name="dsa",
    )(qih, qil, wi, k3, qa, ka, vt)


def _swa_kernel(sink_ref, q_ref, kp_ref, kc_ref, vp_ref, vc_ref, o_ref, *, slopes):
    w = WINDOW
    i = pl.program_id(1)
    grp = B_HEADS // B_KV_HEADS
    row = lax.broadcasted_iota(I32, (w, w), 0)
    col = lax.broadcasted_iota(I32, (w, w), 1)
    valid_p = jnp.logical_and(col > row, i > 0)
    valid_c = col <= row
    dist_p = (w + row - col).astype(F32)
    dist_c = (row - col).astype(F32)
    for kh in range(B_KV_HEADS):
        q2 = q_ref[kh * grp:(kh + 1) * grp].reshape(grp * w, HEAD_DIM) * QK_SCALE
        sp = _dot_nt(q2, kp_ref[kh])
        sc = _dot_nt(q2, kc_ref[kh])
        for g in range(grp):
            h = kh * grp + g
            lp = jnp.where(valid_p, sp[g * w:(g + 1) * w] - slopes[h] * dist_p, NEG_INF)
            lc = jnp.where(valid_c, sc[g * w:(g + 1) * w] - slopes[h] * dist_c, NEG_INF)
            sink = sink_ref[h]
            m = jnp.maximum(jnp.maximum(jnp.max(lp, axis=1, keepdims=True),
                                        jnp.max(lc, axis=1, keepdims=True)), sink)
            pp = jnp.exp(lp - m)
            pc = jnp.exp(lc - m)
            den = (jnp.sum(pp, axis=1, keepdims=True) + jnp.sum(pc, axis=1, keepdims=True)
                   + jnp.exp(sink - m))
            o = _dot(pp.astype(BF16), vp_ref[kh]) + _dot(pc.astype(BF16), vc_ref[kh])
            o_ref[h * HEAD_DIM:(h + 1) * HEAD_DIM, :] = (o / den).T


def _swa(sinks, q, k, v, bn, s_len, slopes):
    nb = s_len // WINDOW
    kv = lambda imap: pl.BlockSpec((B_KV_HEADS, WINDOW, HEAD_DIM), imap)
    cur = lambda b, i: (0, b * nb + i, 0)
    prev = lambda b, i: (0, b * nb + jnp.maximum(i - 1, 0), 0)
    return pl.pallas_call(
        functools.partial(_swa_kernel, slopes=slopes),
        grid=(bn, nb),
        in_specs=[pl.BlockSpec(memory_space=pltpu.SMEM),
                  pl.BlockSpec((B_HEADS, WINDOW, HEAD_DIM), cur), kv(prev), kv(cur), kv(prev), kv(cur)],
        out_specs=pl.BlockSpec((B_HEADS * HEAD_DIM, WINDOW), lambda b, i: (0, b * nb + i)),
        out_shape=jax.ShapeDtypeStruct((B_HEADS * HEAD_DIM, bn * s_len), F32),
        compiler_params=_params("arbitrary", "arbitrary"),
        name="swa",
    )(sinks, q, k, k, v, v)


def _moba_kernel(qh_ref, ql_ref, k_ref, vt_ref, o_ref, kmh_sc, kml_sc, sel_sc, acc_sc,
                 *, nblk, n_top, slopes):
    t = MOBA_BLOCK
    i = pl.program_id(1)
    keyi, qryi = _tile_iotas(t)

    @pl.when(i == 0)
    def _():
        for h in range(C_HEADS):
            km = jnp.sum(k_ref[h].reshape(nblk, t, HEAD_DIM), axis=1) * (1.0 / t)
            kmh, kml = _split_hi_lo(km)
            kmh_sc[h] = kmh
            kml_sc[h] = kml

    blk = lax.broadcasted_iota(I32, (nblk, t), 0)
    for h in range(C_HEADS):
        rows = slice(h * HEAD_DIM, (h + 1) * HEAD_DIM)
        qh, ql = qh_ref[rows, :], ql_ref[rows, :]
        gate = _dot(kmh_sc[h], qh) + (_dot(kmh_sc[h], ql) + _dot(kml_sc[h], qh))
        rank = jnp.zeros((nblk, t), F32)
        for j in range(nblk):
            gj = gate[j:j + 1, :]
            beats = (gj > gate) | ((gj == gate) & (j < blk))
            rank = rank + jnp.where(jnp.logical_and(beats, j < i), 1.0, 0.0)
        sel_sc[h] = jnp.where((rank < n_top) & (blk < i), 1.0, 0.0)

    acc_sc[...] = jnp.zeros(acc_sc.shape, F32)

    def attend(j, ms, ls, mask_of_head):
        start = pl.multiple_of(j * t, t)
        vt = vt_ref[j]
        dist = ((i - j) * t + qryi - keyi).astype(F32)
        rows = [slice(h * HEAD_DIM, (h + 1) * HEAD_DIM) for h in range(C_HEADS)]
        ks = [k_ref[h, pl.ds(start, t), :].astype(BF16) for h in range(C_HEADS)]
        ss = [jnp.where(mask_of_head(h), _dot(ks[h], qh_ref[rows[h], :] * QK_SCALE) - slopes[h] * dist, NEG_INF)
              for h in range(C_HEADS)]
        return _online_softmax_step(ss, ms, ls, acc_sc, [vt[r] for r in rows])

    def past_block(j, carry):
        return attend(j, *carry, lambda h: sel_sc[h, pl.ds(j, 1), :] > 0.5)

    row = lambda v: tuple(jnp.full((1, t), v, F32) for _ in range(C_HEADS))
    ms, ls = lax.fori_loop(0, i, past_block, (row(NEG_INF), row(0.0)))
    _, ls = attend(i, ms, ls, lambda h: keyi <= qryi)
    for h in range(C_HEADS):
        o_ref[h * HEAD_DIM:(h + 1) * HEAD_DIM, :] = acc_sc[h] / ls[h]


def _moba(qh, ql, k, vt, bn, s_len, slopes):
    nblk = s_len // MOBA_BLOCK
    n_top = min(MOBA_TOPK, nblk - 1)
    width = C_HEADS * HEAD_DIM
    return pl.pallas_call(
        functools.partial(_moba_kernel, nblk=nblk, n_top=float(n_top), slopes=slopes),
        grid=(bn, nblk),
        in_specs=[_q_cols_spec(width, nblk), _q_cols_spec(width, nblk),
                  _k_heads_spec(C_HEADS, s_len), _kv_tiles_spec(width, nblk)],
        out_specs=_q_cols_spec(width, nblk),
        out_shape=jax.ShapeDtypeStruct((width, bn * s_len), F32),
        scratch_shapes=[pltpu.VMEM((C_HEADS, nblk, HEAD_DIM), BF16),
                        pltpu.VMEM((C_HEADS, nblk, HEAD_DIM), BF16),
                        pltpu.VMEM((C_HEADS, nblk, MOBA_BLOCK), F32),
                        pltpu.VMEM((C_HEADS, HEAD_DIM, MOBA_BLOCK), F32)],
        compiler_params=_params("arbitrary", "arbitrary"),
        name="moba",
    )(qh, ql, k, vt)


def _stick_kernel(q_ref, k_ref, vt_ref, o_ref, acc_sc):
    t = ATT_TILE
    i = pl.program_id(1)
    keyi, qryi = _tile_iotas(t)
    after = jnp.where(qryi > keyi, 1.0, 0.0).astype(BF16)
    q = q_ref[...] * QK_SCALE
    acc_sc[...] = jnp.zeros(acc_sc.shape, F32)

    def tile(it, later):
        j = i - it
        start = pl.multiple_of(j * t, t)
        strict = ((j - i) * t + keyi) < qryi
        vt = vt_ref[j]
        heads = range(D_HEADS)
        rows = [slice(h * HEAD_DIM, (h + 1) * HEAD_DIM) for h in heads]
        zs = [_dot(k_ref[h, pl.ds(start, t), :], q[rows[h]]) for h in heads]
        sps = [jnp.log1p(jnp.exp(-jnp.abs(z))) for z in zs]
        log_keeps = [jnp.where(strict, -(jnp.maximum(z, 0.0) + sp), 0.0) for z, sp in zip(zs, sps)]
        log_betas = [jnp.minimum(z, 0.0) - sp for z, sp in zip(zs, sps)]
        splits = [_split_hi_lo(lk) for lk in log_keeps]
        log_afters = [later[h] + (_dot(after, splits[h][0]) + _dot(after, splits[h][1])) for h in heads]
        avs = [_dot(vt[rows[h]], jnp.where(strict, jnp.exp(log_betas[h] + log_afters[h]), 0.0).astype(BF16))
               for h in heads]
        for h in heads:
            acc_sc[h] += avs[h]
        return tuple(later[h] + jnp.sum(log_keeps[h], axis=0, keepdims=True) for h in heads)

    lax.fori_loop(0, i + 1, tile, tuple(jnp.zeros((1, t), F32) for _ in range(D_HEADS)))
    for h in range(D_HEADS):
        o_ref[h * HEAD_DIM:(h + 1) * HEAD_DIM, :] = acc_sc[h]


def _stick(q, k, vt, bn, s_len):
    nq = s_len // ATT_TILE
    width = D_HEADS * HEAD_DIM
    return pl.pallas_call(
        _stick_kernel,
        grid=(bn, nq),
        in_specs=[_q_cols_spec(width, nq), _k_heads_spec(D_HEADS, s_len), _kv_tiles_spec(width, nq)],
        out_specs=_q_cols_spec(width, nq),
        out_shape=jax.ShapeDtypeStruct((width, bn * s_len), F32),
        scratch_shapes=[pltpu.VMEM((D_HEADS, HEAD_DIM, ATT_TILE), F32)],
        compiler_params=_params("arbitrary", "arbitrary"),
        name="stick",
    )(q, k, vt)


def _layer_norm(r, g, b):
    mu = jnp.mean(r, axis=-1, keepdims=True)
    d = r - mu
    var = jnp.mean(d * d, axis=-1, keepdims=True)
    return d * lax.rsqrt(var + LN_EPS) * g + b


def _merge_kernel(x_ref, ya_ref, yb_ref, yc_ref, yd_ref, wg_ref, wb_ref, wo_ref, g_ref, b_ref, h_ref):
    x = x_ref[...]
    xb = x.astype(BF16)
    merged = None
    for n, yt_ref in enumerate((ya_ref, yb_ref, yc_ref, yd_ref)):
        gate = jax.nn.sigmoid(_dot(xb, wg_ref[:, n * D_MODEL:(n + 1) * D_MODEL]))
        term = gate * _dot(yt_ref[...].T.astype(BF16), wb_ref[n])
        merged = term if merged is None else merged + term
    mix = _dot(merged.astype(BF16), wo_ref[...])
    h_ref[...] = _layer_norm(DEEPNORM_ALPHA * x + mix, g_ref[...], b_ref[...])


def _merge(x2d, yts, w_gate, w_branch, w_out, g, b, tm):
    t_len = x2d.shape[0]
    row = pl.BlockSpec((tm, D_MODEL), lambda i: (i, 0))
    col = pl.BlockSpec((BRANCH_WIDTH, tm), lambda i: (0, i))
    full = lambda shape: pl.BlockSpec(shape, lambda i: (0,) * len(shape))
    return pl.pallas_call(
        _merge_kernel,
        grid=(t_len // tm,),
        in_specs=[row] + [col] * N_BRANCH
                 + [full(w_gate.shape), full(w_branch.shape), full(w_out.shape),
                    full((1, D_MODEL)), full((1, D_MODEL))],
        out_specs=row,
        out_shape=jax.ShapeDtypeStruct((t_len, D_MODEL), F32),
        compiler_params=_params("arbitrary"),
        name="merge_ln",
    )(x2d, *yts, w_gate, w_branch, w_out, g.reshape(1, D_MODEL), b.reshape(1, D_MODEL))


def _top16_rows(s, vals_sc):
    n, tm = s.shape
    rows = lax.broadcasted_iota(I32, (n, tm), 0).astype(F32)
    rank = jnp.full((n, tm), float(PEER_TOPK), F32)
    work = s
    for k in range(PEER_TOPK):
        m = jnp.max(work, axis=0, keepdims=True)
        first = jnp.min(jnp.where(work == m, rows, float(n)), axis=0, keepdims=True)
        hit = rows == first
        rank = jnp.where(hit, float(k), rank)
        work = jnp.where(hit, -jnp.inf, work)
        vals_sc[k:k + 1, :] = m
    return rank


def _peer_route_kernel(h_ref, wqh_ref, wql_ref, wsh_ref, wsl_ref,
                       rank2_ref, r_ref, e2_ref, e1n_ref, a_sc, b_sc, c_sc, top_sc):
    tm = h_ref.shape[0]
    hh, hl = _split_hi_lo(h_ref[...])
    qt = _dot_nt(wqh_ref[...], hh) + (_dot_nt(wql_ref[...], hh) + _dot_nt(wqh_ref[...], hl))
    for h in range(PEER_HEADS):
        qh, ql = _split_hi_lo(qt[h * PEER_QDIM:(h + 1) * PEER_QDIM, :])
        st = _dot(wsh_ref[h], qh) + (_dot(wsl_ref[h], qh) + _dot(wsh_ref[h], ql))
        s1, s2 = st[:N_KEYS], st[N_KEYS:]
        rank1 = _top16_rows(s1, a_sc)
        rank2 = _top16_rows(s2, b_sc)
        a16, b16 = a_sc[...], b_sc[...]
        c_sc[0:16, :] = a16[0:1] + b16
        for ia in range(1, 8):
            c_sc[8 + 8 * ia:16 + 8 * ia, :] = a16[ia:ia + 1] + b16[0:8]
        c_sc[72:80, :] = a16[8:16] + b16[0:1]
        _top16_rows(c_sc[...], top_sc)
        top = top_sc[...]
        tau = top[PEER_TOPK - 1:PEER_TOPK]
        z = jnp.sum(jnp.exp(top - top[0:1]), axis=0, keepdims=True)
        r = jnp.zeros((N_KEYS, tm), F32)
        for j in range(PEER_TOPK):
            r = r + jnp.where((s1 + b16[j:j + 1]) >= tau, 1.0, 0.0)
        r_ref[h] = jnp.where(rank1 < PEER_TOPK, r, 0.0)
        rank2_ref[h] = rank2.astype(BF16)
        e2_ref[h] = jnp.exp(s2 - b16[0:1]).astype(BF16)
        e1n_ref[h] = jnp.exp(s1 - a16[0:1]) / z


def _peer_route(h2d, wq_t, wsub_t, tm):
    t_len = h2d.shape[0]
    wqh, wql = _split_hi_lo(wq_t)
    wsh, wsl = _split_hi_lo(wsub_t)
    full = lambda shape: pl.BlockSpec(shape, lambda i: (0,) * len(shape))
    rows_spec = pl.BlockSpec((PEER_HEADS, N_KEYS, tm), lambda i: (0, 0, i))
    rows_shape = jax.ShapeDtypeStruct((PEER_HEADS, N_KEYS, t_len), F32)
    tile_spec = rows_spec
    tile_shape = jax.ShapeDtypeStruct((PEER_HEADS, N_KEYS, t_len), BF16)
    return pl.pallas_call(
        _peer_route_kernel,
        grid=(t_len // tm,),
        in_specs=[pl.BlockSpec((tm, D_MODEL), lambda i: (i, 0)),
                  full(wqh.shape), full(wql.shape), full(wsh.shape), full(wsl.shape)],
        out_specs=[tile_spec, rows_spec, tile_spec, rows_spec],
        out_shape=[tile_shape, rows_shape, tile_shape, rows_shape],
        scratch_shapes=[pltpu.VMEM((PEER_TOPK, tm), F32), pltpu.VMEM((PEER_TOPK, tm), F32),
                        pltpu.VMEM((80, tm), F32), pltpu.VMEM((PEER_TOPK, tm), F32)],
        compiler_params=_params("arbitrary"),
        name="peer_route",
    )(h2d, wqh, wql, wsh, wsl)


def _packed_rows(row):
    return jnp.broadcast_to(row, (BF16_ROWS, row.shape[1])).astype(BF16)


def _peer_dense_kernel(h_ref, u_ref, vt_ref, rank2_ref, r_ref, e2_ref, e1n_ref, g_ref, b_ref,
                       o_ref, xb_sc, ht_sc, p_sc, acc_sc, *, te):
    j = pl.program_id(1)
    tm = h_ref.shape[0]
    rows_per_step = te // N_KEYS

    @pl.when(j == 0)
    def _():
        acc_sc[...] = jnp.zeros(acc_sc.shape, F32)
        xb_sc[...] = h_ref[...].astype(BF16)

    ht_sc[...] = _dot_nt(u_ref[...], xb_sc[...])
    for a in range(rows_per_step):
        i1 = j * rows_per_step + a
        rs = slice(a * N_KEYS, (a + 1) * N_KEYS)
        r_rows = [_packed_rows(r_ref[h, pl.ds(i1, 1), :]) for h in range(PEER_HEADS)]
        w_rows = [_packed_rows(e1n_ref[h, pl.ds(i1, 1), :]) for h in range(PEER_HEADS)]
        tiles = []
        for c in range(tm // BF16_MIN_LANES):
            cs = slice(c * BF16_MIN_LANES, (c + 1) * BF16_MIN_LANES)
            gsum = None
            for h in range(PEER_HEADS):
                r_tile = jnp.tile(r_rows[h][:, cs], (N_KEYS // BF16_ROWS, 1))
                w_tile = jnp.tile(w_rows[h][:, cs], (N_KEYS // BF16_ROWS, 1))
                term = jnp.where(rank2_ref[h, :, cs] < r_tile, e2_ref[h, :, cs] * w_tile, 0.0)
                gsum = term if gsum is None else gsum + term
            ht = ht_sc[rs, cs]
            act = 0.5 * ht * (1.0 + lax.erf(ht * float(np.sqrt(0.5))))
            tiles.append(gsum * act.astype(BF16))
        p_sc[rs, :] = jnp.concatenate(tiles, axis=1)
    acc_sc[...] += _dot(vt_ref[...], p_sc[...])

    @pl.when(j == pl.num_programs(1) - 1)
    def _():
        r = DEEPNORM_ALPHA * h_ref[...] + acc_sc[...].T
        o_ref[...] = _layer_norm(r, g_ref[...], b_ref[...])


def _peer_dense(h2d, u_bf, vt_bf, rank2, r, e2, e1n, g, b, tm, te):
    t_len = h2d.shape[0]
    rspec = pl.BlockSpec((PEER_HEADS, N_KEYS, tm), lambda i, j: (0, 0, i))
    tspec = rspec
    vec = pl.BlockSpec((1, D_MODEL), lambda i, j: (0, 0))
    return pl.pallas_call(
        functools.partial(_peer_dense_kernel, te=te),
        grid=(t_len // tm, N_EXPERTS // te),
        in_specs=[pl.BlockSpec((tm, D_MODEL), lambda i, j: (i, 0)),
                  pl.BlockSpec((te, D_MODEL), lambda i, j: (j, 0)),
                  pl.BlockSpec((D_MODEL, te), lambda i, j: (0, j)),
                  tspec, rspec, tspec, rspec, vec, vec],
        out_specs=pl.BlockSpec((tm, D_MODEL), lambda i, j: (i, 0)),
        out_shape=jax.ShapeDtypeStruct((t_len, D_MODEL), F32),
        scratch_shapes=[pltpu.VMEM((tm, D_MODEL), BF16), pltpu.VMEM((te, tm), F32),
                        pltpu.VMEM((te, tm), BF16), pltpu.VMEM((D_MODEL, tm), F32)],
        compiler_params=_params("arbitrary", "arbitrary"),
        name="peer_dense",
    )(h2d, u_bf, vt_bf, rank2, r, e2, e1n, g.reshape(1, D_MODEL), b.reshape(1, D_MODEL))


def _sub_key_blocks(subkeys):
    hn, _, nk, d = subkeys.shape
    z = jnp.zeros((hn, nk, d), subkeys.dtype)
    top = jnp.concatenate([subkeys[:, 0], z], axis=-1)
    bot = jnp.concatenate([z, subkeys[:, 1]], axis=-1)
    return jnp.concatenate([top, bot], axis=1)


_PROJ_GROUPS = (
    ("cols", "bf16"), ("rows", "bf16"), ("coltiles", "bf16"),
    ("cols", "hilo"), ("rows", "hilo"), ("cols", "f32"),
    ("heads", "bf16"), ("heads", "bf16"), ("heads", "bf16"),
    ("cols", "hilo"), ("heads", "f32"), ("coltiles", "bf16"),
    ("cols", "bf16"), ("heads", "bf16"), ("coltiles", "bf16"),
)


def _token_mixer(h2d, bn, s_len, w_in, w_branch, w_out, sinks, ln_g, ln_b, slopes):
    offs = np.concatenate([[0], np.cumsum(IN_WIDTHS)])
    cols = [w_in[:, offs[k]:offs[k + 1]] for k in range(len(IN_WIDTHS))]
    groups = [(cols[k], layout, prec) for k, (layout, prec) in enumerate(_PROJ_GROUPS)]
    (qa, ka, va, qi_h, qi_l, ki_h, ki_l, wi, qb, kb, vb, qc_h, qc_l, kc, vc, qd, kd, vd) = _project(
        h2d, groups, tm=PROJ_TM, kt=ATT_TILE)
    k3 = jnp.concatenate([ki_h, ki_h, ki_l], axis=-1)
    yts = [_dsa(qi_h, qi_l, wi, k3, qa, ka, va, bn, s_len, slopes[0]),
           _swa(sinks, qb, kb, vb, bn, s_len, slopes[1]),
           _moba(qc_h, qc_l, kc, vc, bn, s_len, slopes[2]),
           _stick(qd, kd, vd, bn, s_len)]
    return _merge(h2d, yts, cols[15].astype(BF16), w_branch.astype(BF16), w_out.astype(BF16),
                  ln_g, ln_b, tm=MERGE_TM)


def _peer(h2d, wq, subkeys, u, v, ln_g, ln_b):
    rank2, r, e2, e1n = _peer_route(h2d, wq.T, _sub_key_blocks(subkeys), tm=ROUTE_TM)
    return _peer_dense(h2d, u.astype(BF16), v.T.astype(BF16), rank2, r, e2, e1n, ln_g, ln_b,
                       tm=PEER_TM, te=PEER_TE)


def kernel(x, w_in, w_branch, w_out, attn_sinks, ln1_g, ln1_b, peer_wq, peer_subkeys,
           peer_u, peer_v, ln2_g, ln2_b):
    bn, s_len, d = x.shape
    slopes = _alibi_slopes()
    h = x.reshape(bn * s_len, d)
    for l in range(DEPTH):
        h = _token_mixer(h, bn, s_len, w_in[l], w_branch[l], w_out[l], attn_sinks[l],
                         ln1_g[l], ln1_b[l], slopes)
        h = _peer(h, peer_wq[l], peer_subkeys[l], peer_u[l], peer_v[l], ln2_g[l], ln2_b[l])
    return h.reshape(bn, s_len, d)
```

```python
import functools

import jax
import jax.numpy as jnp
import numpy as np
from jax import lax
from jax.experimental import pallas as pl
from jax.experimental.pallas import tpu as pltpu

F32 = jnp.float32
BF16 = jnp.bfloat16
I32 = jnp.int32

D_MODEL = 1024
DEPTH = 2
HEAD_DIM = 64
A_HEADS = 4
IDX_HEADS = 8
IDX_DIM = 32
DSA_TOPK_MAX = 256
B_HEADS = 4
B_KV_HEADS = 2
WINDOW = 128
C_HEADS = 4
MOBA_BLOCK = 256
MOBA_TOPK = 3
D_HEADS = 4
N_BRANCH = 4
BRANCH_WIDTH = 4 * HEAD_DIM
PEER_HEADS = 8
N_KEYS = 128
N_EXPERTS = N_KEYS * N_KEYS
PEER_QDIM = 128
PEER_TOPK = 16
LN_EPS = 1e-5
NEG_INF = -1e30
DEEPNORM_ALPHA = (2 * DEPTH) ** 0.25
IDX_SCALE = (IDX_DIM * IDX_HEADS) ** -0.5
QK_SCALE = HEAD_DIM ** -0.5

IN_WIDTHS = (
    A_HEADS * HEAD_DIM, HEAD_DIM, HEAD_DIM,
    IDX_HEADS * IDX_DIM, IDX_DIM, IDX_HEADS,
    B_HEADS * HEAD_DIM, B_KV_HEADS * HEAD_DIM, B_KV_HEADS * HEAD_DIM,
    C_HEADS * HEAD_DIM, C_HEADS * HEAD_DIM, C_HEADS * HEAD_DIM,
    D_HEADS * HEAD_DIM, D_HEADS * HEAD_DIM, D_HEADS * HEAD_DIM,
    N_BRANCH * D_MODEL,
)

V7X_LANES = 128
V7X_SUBLANES = 8
BF16_ROWS = 2 * V7X_SUBLANES
BF16_MIN_LANES = 2 * V7X_LANES
V7X_VMEM_BYTES = 64 * 1024 * 1024
VMEM_LIMIT_BYTES = (V7X_VMEM_BYTES * 3) // 4

ATT_TILE = 256
PROJ_TM = 512
MERGE_TM = 256
ROUTE_TM = 256
PEER_TM = 512
PEER_TE = 1024
PEER_ROWS_PER_MATMUL = 2

INT_MIN = -(2 ** 31)

_NT = (((1,), (1,)), ((), ()))


def _alibi_slopes():
    n = A_HEADS + B_HEADS + C_HEADS
    s = [2.0 ** (-8.0 * k / n) for k in range(1, n + 1)]
    return tuple(s[0::3]), tuple(s[1::3]), tuple(s[2::3])


def _dot(a, b):
    return jnp.dot(a, b, preferred_element_type=F32)


def _dot_nt(a, b):
    return lax.dot_general(a, b, _NT, preferred_element_type=F32)


def _split_hi_lo(x):
    hi = x.astype(BF16)
    lo = (x - hi.astype(F32)).astype(BF16)
    return hi, lo


def _params(*sem):
    return pltpu.CompilerParams(dimension_semantics=sem, vmem_limit_bytes=VMEM_LIMIT_BYTES)


def _tile_iotas(n):
    return lax.broadcasted_iota(I32, (n, n), 0), lax.broadcasted_iota(I32, (n, n), 1)


def _proj_kernel(groups, kt, x_ref, *refs):
    n_w = sum(1 if prec == "bf16" else 2 for _, prec in groups)
    w_refs, o_refs = refs[:n_w], refs[n_w:]
    xh, xl = _split_hi_lo(x_ref[...])
    wi = oi = 0
    for layout, prec in groups:
        n_in = 1 if prec == "bf16" else 2
        ws = w_refs[wi:wi + n_in]
        wi += n_in
        n_out = 2 if prec == "hilo" else 1
        outs = o_refs[oi:oi + n_out]
        oi += n_out
        transposed = layout in ("cols", "coltiles")

        def result(pick):
            mm = (lambda w, x: _dot_nt(w, x)) if transposed else (lambda w, x: _dot(x, w))
            wh = pick(ws[0])
            acc = mm(wh, xh)
            if prec != "bf16":
                acc = acc + (mm(pick(ws[1]), xh) + mm(wh, xl))
            return acc

        def emit(store, acc):
            if prec == "bf16":
                store(outs[0], acc.astype(BF16))
            elif prec == "f32":
                store(outs[0], acc)
            else:
                hi, lo = _split_hi_lo(acc)
                store(outs[0], hi)
                store(outs[1], lo)

        if layout in ("rows", "cols"):
            def store_all(ref, val):
                ref[...] = val
            emit(store_all, result(lambda r: r[...]))
        elif layout == "coltiles":
            acc = result(lambda r: r[...])
            for c in range(acc.shape[1] // kt):
                def store_tile(ref, val, c=c):
                    ref[c] = val
                emit(store_tile, acc[:, c * kt:(c + 1) * kt])
        else:
            acc = result(lambda r: r[...])
            for h in range(acc.shape[1] // HEAD_DIM):
                def store_head(ref, val, h=h):
                    ref[h] = val
                emit(store_head, acc[:, h * HEAD_DIM:(h + 1) * HEAD_DIM])


def _project(x2d, groups, tm, kt):
    t_len, kdim = x2d.shape
    ins, in_specs, out_shapes, out_specs = [x2d], [pl.BlockSpec((tm, kdim), lambda i: (i, 0))], [], []
    for w, layout, prec in groups:
        n = w.shape[1]
        wk = w.T if layout in ("cols", "coltiles") else w
        parts = [wk.astype(BF16)] if prec == "bf16" else list(_split_hi_lo(wk))
        for p in parts:
            ins.append(p)
            in_specs.append(pl.BlockSpec(p.shape, lambda i, nd=p.ndim: (0,) * nd))
        odt = F32 if prec == "f32" else BF16
        if layout == "rows":
            shape, spec = (t_len, n), pl.BlockSpec((tm, n), lambda i: (i, 0))
        elif layout == "cols":
            shape, spec = (n, t_len), pl.BlockSpec((n, tm), lambda i: (0, i))
        elif layout == "coltiles":
            shape, spec = (t_len // kt, n, kt), pl.BlockSpec((tm // kt, n, kt), lambda i: (i, 0, 0))
        else:
            nh = n // HEAD_DIM
            shape, spec = (nh, t_len, HEAD_DIM), pl.BlockSpec((nh, tm, HEAD_DIM), lambda i: (0, i, 0))
        for _ in range(2 if prec == "hilo" else 1):
            out_shapes.append(jax.ShapeDtypeStruct(shape, odt))
            out_specs.append(spec)
    return pl.pallas_call(
        functools.partial(_proj_kernel, tuple((l, p) for _, l, p in groups), kt),
        grid=(t_len // tm,),
        in_specs=in_specs, out_specs=out_specs, out_shape=out_shapes,
        compiler_params=_params("arbitrary"),
        name="in_proj",
    )(*ins)


def _q_cols_spec(n, nq):
    return pl.BlockSpec((n, ATT_TILE), lambda b, i: (0, b * nq + i))


def _kv_tiles_spec(n, nk):
    return pl.BlockSpec((nk, n, ATT_TILE), lambda b, i: (b, 0, 0))


def _k_heads_spec(nh, s_len):
    return pl.BlockSpec((nh, s_len, HEAD_DIM), lambda b, i: (0, b, 0))


def _online_softmax_step(ss, ms, ls, acc_ref, vts):
    m_next = [jnp.maximum(m, jnp.max(s, axis=0, keepdims=True)) for s, m in zip(ss, ms)]
    ps = [jnp.exp(s - m) for s, m in zip(ss, m_next)]
    alphas = [jnp.exp(m0 - m1) for m0, m1 in zip(ms, m_next)]
    pvs = [_dot(vt, p.astype(BF16)) for vt, p in zip(vts, ps)]
    l_next = [a * l + jnp.sum(p, axis=0, keepdims=True) for a, l, p in zip(alphas, ls, ps)]
    for h in range(len(ss)):
        acc_ref[h] = acc_ref[h] * alphas[h] + pvs[h]
    return tuple(m_next), tuple(l_next)


def _dsa_kernel(qih_ref, qil_ref, wi_ref, k3_ref, qa_ref, ka_ref, vt_ref, o_ref,
                keys_sc, acc_sc, *, n_sel, slopes):
    t = ATT_TILE
    i = pl.program_id(1)
    nch = i + 1
    keyi, qryi = _tile_iotas(t)
    wi = wi_ref[...]
    q3 = []
    for h in range(IDX_HEADS):
        rows = slice(h * IDX_DIM, (h + 1) * IDX_DIM)
        q3.append(jnp.concatenate([qih_ref[rows, :], qil_ref[rows, :], qih_ref[rows, :]], axis=0))

    def score_tile(c, carry):
        k3 = k3_ref[pl.ds(pl.multiple_of(c * t, t), t), :]
        sc = None
        for h in range(IDX_HEADS):
            r = jnp.maximum(_dot(k3, q3[h]), 0.0) * wi[h:h + 1, :]
            sc = r if sc is None else sc + r
        sc = sc * IDX_SCALE
        sc = jnp.where(sc == 0.0, 0.0, sc)
        bits = lax.bitcast_convert_type(sc, I32)
        key = bits ^ ((bits >> 31) & 0x7FFFFFFF)
        valid = ((c - i) * t + keyi) <= qryi
        keys_sc[c] = jnp.where(valid, key, INT_MIN)
        return carry

    lax.fori_loop(0, nch, score_tile, 0)

    def count(pred):
        def body(c, acc):
            ind = jnp.where(pred(keys_sc[c]), 1.0, 0.0)
            return acc + jnp.sum(ind.reshape(t // V7X_SUBLANES, V7X_SUBLANES, t), axis=0)
        acc = lax.fori_loop(0, nch, body, jnp.zeros((V7X_SUBLANES, t), F32))
        return jnp.sum(acc, axis=0, keepdims=True)

    def bit_step(it, tau):
        cand = tau ^ lax.shift_left(jnp.int32(1), 31 - it)
        cnt = count(lambda k: k >= cand)
        return jnp.where(cnt >= n_sel, cand, tau)

    tau = lax.fori_loop(0, 32, bit_step, jnp.full((1, t), INT_MIN, I32))
    need = n_sel - count(lambda k: k > tau)

    acc_sc[...] = jnp.zeros(acc_sc.shape, F32)
    before = jnp.where(qryi < keyi, 1.0, 0.0).astype(BF16)
    qa = qa_ref[...] * QK_SCALE

    def attend_tile(c, carry):
        n_eq_before, ms, ls = carry
        start = pl.multiple_of(c * t, t)
        key = keys_sc[c]
        eq = key == tau
        eqf = jnp.where(eq, 1.0, 0.0)
        taken = (n_eq_before + _dot(before, eqf.astype(BF16))) < need
        bias = jnp.where(key > tau, 0.0, jnp.where(eq, jnp.where(taken, 0.0, NEG_INF), NEG_INF))
        bias = jnp.where(((c - i) * t + keyi) <= qryi, bias, NEG_INF)
        k = ka_ref[pl.ds(start, t), :]
        vt = vt_ref[c]
        dist = ((i - c) * t + qryi - keyi).astype(F32)
        ss = [_dot(k, qa[h * HEAD_DIM:(h + 1) * HEAD_DIM]) - slopes[h] * dist + bias for h in range(A_HEADS)]
        ms, ls = _online_softmax_step(ss, ms, ls, acc_sc, [vt] * A_HEADS)
        return n_eq_before + jnp.sum(eqf, axis=0, keepdims=True), ms, ls

    row = lambda v: tuple(jnp.full((1, t), v, F32) for _ in range(A_HEADS))
    _, _, ls = lax.fori_loop(0, nch, attend_tile, (jnp.zeros((1, t), F32), row(NEG_INF), row(0.0)))
    for h in range(A_HEADS):
        o_ref[h * HEAD_DIM:(h + 1) * HEAD_DIM, :] = acc_sc[h] / ls[h]


def _dsa(qih, qil, wi, k3, qa, ka, vt, bn, s_len, slopes):
    nq = s_len // ATT_TILE
    n_sel = min(DSA_TOPK_MAX, s_len // 4)
    return pl.pallas_call(
        functools.partial(_dsa_kernel, n_sel=float(n_sel), slopes=slopes),
        grid=(bn, nq),
        in_specs=[_q_cols_spec(IDX_HEADS * IDX_DIM, nq), _q_cols_spec(IDX_HEADS * IDX_DIM, nq),
                  _q_cols_spec(IDX_HEADS, nq),
                  pl.BlockSpec((s_len, 3 * IDX_DIM), lambda b, i: (b, 0)),
                  _q_cols_spec(A_HEADS * HEAD_DIM, nq),
                  pl.BlockSpec((s_len, HEAD_DIM), lambda b, i: (b, 0)),
                  _kv_tiles_spec(HEAD_DIM, nq)],
        out_specs=_q_cols_spec(A_HEADS * HEAD_DIM, nq),
        out_shape=jax.ShapeDtypeStruct((A_HEADS * HEAD_DIM, bn * s_len), F32),
        scratch_shapes=[pltpu.VMEM((nq, ATT_TILE, ATT_TILE), I32),
                        pltpu.VMEM((A_HEADS, HEAD_DIM, ATT_TILE), F32)],
        compiler_params=_params("arbitrary", "arbitrary"),
        name="dsa",
    )(qih, qil, wi, k3, qa, ka, vt)


def _swa_kernel(sink_ref, q_ref, kp_ref, kc_ref, vp_ref, vc_ref, o_ref, *, slopes):
    w = WINDOW
    i = pl.program_id(1)
    grp = B_HEADS // B_KV_HEADS
    row = lax.broadcasted_iota(I32, (w, w), 0)
    col = lax.broadcasted_iota(I32, (w, w), 1)
    valid_p = jnp.logical_and(col > row, i > 0)
    valid_c = col <= row
    dist_p = (w + row - col).astype(F32)
    dist_c = (row - col).astype(F32)
    for kh in range(B_KV_HEADS):
        q2 = q_ref[kh * grp:(kh + 1) * grp].reshape(grp * w, HEAD_DIM) * QK_SCALE
        sp = _dot_nt(q2, kp_ref[kh])
        sc = _dot_nt(q2, kc_ref[kh])
        for g in range(grp):
            h = kh * grp + g
            lp = jnp.where(valid_p, sp[g * w:(g + 1) * w] - slopes[h] * dist_p, NEG_INF)
            lc = jnp.where(valid_c, sc[g * w:(g + 1) * w] - slopes[h] * dist_c, NEG_INF)
            sink = sink_ref[h]
            m = jnp.maximum(jnp.maximum(jnp.max(lp, axis=1, keepdims=True),
                                        jnp.max(lc, axis=1, keepdims=True)), sink)
            pp = jnp.exp(lp - m)
            pc = jnp.exp(lc - m)
            den = (jnp.sum(pp, axis=1, keepdims=True) + jnp.sum(pc, axis=1, keepdims=True)
                   + jnp.exp(sink - m))
            o = _dot(pp.astype(BF16), vp_ref[kh]) + _dot(pc.astype(BF16), vc_ref[kh])
            o_ref[h * HEAD_DIM:(h + 1) * HEAD_DIM, :] = (o / den).T


def _swa(sinks, q, k, v, bn, s_len, slopes):
    nb = s_len // WINDOW
    kv = lambda imap: pl.BlockSpec((B_KV_HEADS, WINDOW, HEAD_DIM), imap)
    cur = lambda b, i: (0, b * nb + i, 0)
    prev = lambda b, i: (0, b * nb + jnp.maximum(i - 1, 0), 0)
    return pl.pallas_call(
        functools.partial(_swa_kernel, slopes=slopes),
        grid=(bn, nb),
        in_specs=[pl.BlockSpec(memory_space=pltpu.SMEM),
                  pl.BlockSpec((B_HEADS, WINDOW, HEAD_DIM), cur), kv(prev), kv(cur), kv(prev), kv(cur)],
        out_specs=pl.BlockSpec((B_HEADS * HEAD_DIM, WINDOW), lambda b, i: (0, b * nb + i)),
        out_shape=jax.ShapeDtypeStruct((B_HEADS * HEAD_DIM, bn * s_len), F32),
        compiler_params=_params("arbitrary", "arbitrary"),
        name="swa",
    )(sinks, q, k, k, v, v)


def _moba_kernel(qh_ref, ql_ref, k_ref, vt_ref, o_ref, kmh_sc, kml_sc, sel_sc, acc_sc,
                 *, nblk, n_top, slopes):
    t = MOBA_BLOCK
    i = pl.program_id(1)
    keyi, qryi = _tile_iotas(t)

    @pl.when(i == 0)
    def _():
        for h in range(C_HEADS):
            km = jnp.sum(k_ref[h].reshape(nblk, t, HEAD_DIM), axis=1) * (1.0 / t)
            kmh, kml = _split_hi_lo(km)
            kmh_sc[h] = kmh
            kml_sc[h] = kml

    blk = lax.broadcasted_iota(I32, (nblk, t), 0)
    for h in range(C_HEADS):
        rows = slice(h * HEAD_DIM, (h + 1) * HEAD_DIM)
        qh, ql = qh_ref[rows, :], ql_ref[rows, :]
        gate = _dot(kmh_sc[h], qh) + (_dot(kmh_sc[h], ql) + _dot(kml_sc[h], qh))
        rank = jnp.zeros((nblk, t), F32)
        for j in range(nblk):
            gj = gate[j:j + 1, :]
            beats = (gj > gate) | ((gj == gate) & (j < blk))
            rank = rank + jnp.where(jnp.logical_and(beats, j < i), 1.0, 0.0)
        sel_sc[h] = jnp.where((rank < n_top) & (blk < i), 1.0, 0.0)

    acc_sc[...] = jnp.zeros(acc_sc.shape, F32)

    def attend(j, ms, ls, mask_of_head):
        start = pl.multiple_of(j * t, t)
        vt = vt_ref[j]
        dist = ((i - j) * t + qryi - keyi).astype(F32)
        rows = [slice(h * HEAD_DIM, (h + 1) * HEAD_DIM) for h in range(C_HEADS)]
        ks = [k_ref[h, pl.ds(start, t), :].astype(BF16) for h in range(C_HEADS)]
        ss = [jnp.where(mask_of_head(h), _dot(ks[h], qh_ref[rows[h], :] * QK_SCALE) - slopes[h] * dist, NEG_INF)
              for h in range(C_HEADS)]
        return _online_softmax_step(ss, ms, ls, acc_sc, [vt[r] for r in rows])

    def past_block(j, carry):
        return attend(j, *carry, lambda h: sel_sc[h, pl.ds(j, 1), :] > 0.5)

    row = lambda v: tuple(jnp.full((1, t), v, F32) for _ in range(C_HEADS))
    ms, ls = lax.fori_loop(0, i, past_block, (row(NEG_INF), row(0.0)))
    _, ls = attend(i, ms, ls, lambda h: keyi <= qryi)
    for h in range(C_HEADS):
        o_ref[h * HEAD_DIM:(h + 1) * HEAD_DIM, :] = acc_sc[h] / ls[h]


def _moba(qh, ql, k, vt, bn, s_len, slopes):
    nblk = s_len // MOBA_BLOCK
    n_top = min(MOBA_TOPK, nblk - 1)
    width = C_HEADS * HEAD_DIM
    return pl.pallas_call(
        functools.partial(_moba_kernel, nblk=nblk, n_top=float(n_top), slopes=slopes),
        grid=(bn, nblk),
        in_specs=[_q_cols_spec(width, nblk), _q_cols_spec(width, nblk),
                  _k_heads_spec(C_HEADS, s_len), _kv_tiles_spec(width, nblk)],
        out_specs=_q_cols_spec(width, nblk),
        out_shape=jax.ShapeDtypeStruct((width, bn * s_len), F32),
        scratch_shapes=[pltpu.VMEM((C_HEADS, nblk, HEAD_DIM), BF16),
                        pltpu.VMEM((C_HEADS, nblk, HEAD_DIM), BF16),
                        pltpu.VMEM((C_HEADS, nblk, MOBA_BLOCK), F32),
                        pltpu.VMEM((C_HEADS, HEAD_DIM, MOBA_BLOCK), F32)],
        compiler_params=_params("arbitrary", "arbitrary"),
        name="moba",
    )(qh, ql, k, vt)


def _stick_kernel(q_ref, k_ref, vt_ref, o_ref, acc_sc):
    t = ATT_TILE
    i = pl.program_id(1)
    keyi, qryi = _tile_iotas(t)
    after = jnp.where(qryi > keyi, 1.0, 0.0).astype(BF16)
    q = q_ref[...] * QK_SCALE
    acc_sc[...] = jnp.zeros(acc_sc.shape, F32)

    def tile(j, later, diagonal):
        start = pl.multiple_of(j * t, t)
        strict = keyi < qryi
        mask = (lambda x: jnp.where(strict, x, 0.0)) if diagonal else (lambda x: x)
        vt = vt_ref[j]
        heads = range(D_HEADS)
        rows = [slice(h * HEAD_DIM, (h + 1) * HEAD_DIM) for h in heads]
        zs = [_dot(k_ref[h, pl.ds(start, t), :], q[rows[h]]) for h in heads]
        sps = [jnp.log(1.0 + jnp.exp(-jnp.abs(z))) for z in zs]
        log_keeps = [mask(-(jnp.maximum(z, 0.0) + sp)) for z, sp in zip(zs, sps)]
        log_betas = [jnp.minimum(z, 0.0) - sp for z, sp in zip(zs, sps)]
        splits = [_split_hi_lo(lk) for lk in log_keeps]
        log_afters = [later[h] + (_dot(after, splits[h][0]) + _dot(after, splits[h][1])) for h in heads]
        avs = [_dot(vt[rows[h]], mask(jnp.exp(log_betas[h] + log_afters[h])).astype(BF16)) for h in heads]
        for h in heads:
            acc_sc[h] += avs[h]
        return tuple(later[h] + jnp.sum(log_keeps[h], axis=0, keepdims=True) for h in heads)

    later = tile(i, tuple(jnp.zeros((1, t), F32) for _ in range(D_HEADS)), diagonal=True)
    lax.fori_loop(0, i, lambda it, later: tile(i - 1 - it, later, diagonal=False), later)
    for h in range(D_HEADS):
        o_ref[h * HEAD_DIM:(h + 1) * HEAD_DIM, :] = acc_sc[h]


def _stick(q, k, vt, bn, s_len):
    nq = s_len // ATT_TILE
    width = D_HEADS * HEAD_DIM
    return pl.pallas_call(
        _stick_kernel,
        grid=(bn, nq),
        in_specs=[_q_cols_spec(width, nq), _k_heads_spec(D_HEADS, s_len), _kv_tiles_spec(width, nq)],
        out_specs=_q_cols_spec(width, nq),
        out_shape=jax.ShapeDtypeStruct((width, bn * s_len), F32),
        scratch_shapes=[pltpu.VMEM((D_HEADS, HEAD_DIM, ATT_TILE), F32)],
        compiler_params=_params("arbitrary", "arbitrary"),
        name="stick",
    )(q, k, vt)


def _layer_norm(r, g, b):
    mu = jnp.mean(r, axis=-1, keepdims=True)
    d = r - mu
    var = jnp.mean(d * d, axis=-1, keepdims=True)
    return d * lax.rsqrt(var + LN_EPS) * g + b


def _merge_kernel(x_ref, ya_ref, yb_ref, yc_ref, yd_ref, wg_ref, wb_ref, wo_ref, g_ref, b_ref, h_ref):
    x = x_ref[...]
    xb = x.astype(BF16)
    merged = None
    for n, yt_ref in enumerate((ya_ref, yb_ref, yc_ref, yd_ref)):
        gate = jax.nn.sigmoid(_dot(xb, wg_ref[:, n * D_MODEL:(n + 1) * D_MODEL]))
        term = gate * _dot(yt_ref[...].T.astype(BF16), wb_ref[n])
        merged = term if merged is None else merged + term
    mix = _dot(merged.astype(BF16), wo_ref[...])
    h_ref[...] = _layer_norm(DEEPNORM_ALPHA * x + mix, g_ref[...], b_ref[...])


def _merge(x2d, yts, w_gate, w_branch, w_out, g, b, tm):
    t_len = x2d.shape[0]
    row = pl.BlockSpec((tm, D_MODEL), lambda i: (i, 0))
    col = pl.BlockSpec((BRANCH_WIDTH, tm), lambda i: (0, i))
    full = lambda shape: pl.BlockSpec(shape, lambda i: (0,) * len(shape))
    return pl.pallas_call(
        _merge_kernel,
        grid=(t_len // tm,),
        in_specs=[row] + [col] * N_BRANCH
                 + [full(w_gate.shape), full(w_branch.shape), full(w_out.shape),
                    full((1, D_MODEL)), full((1, D_MODEL))],
        out_specs=row,
        out_shape=jax.ShapeDtypeStruct((t_len, D_MODEL), F32),
        compiler_params=_params("arbitrary"),
        name="merge_ln",
    )(x2d, *yts, w_gate, w_branch, w_out, g.reshape(1, D_MODEL), b.reshape(1, D_MODEL))


def _top16_rows(s, vals_sc, store_rank=None, store_taken=None):
    n, tm = s.shape

    def extract(first_only):
        rank = jnp.full((n, tm), float(PEER_TOPK), F32)
        work = s
        for k in range(PEER_TOPK):
            m = jnp.max(work, axis=0, keepdims=True)
            hit = work == m
            if first_only:
                rows = lax.broadcasted_iota(I32, (n, tm), 0).astype(F32)
                hit = rows == jnp.min(jnp.where(hit, rows, float(n)), axis=0, keepdims=True)
            if store_rank is not None:
                rank = jnp.where(hit, float(k), rank)
            work = jnp.where(hit, -jnp.inf, work)
            vals_sc[k:k + 1, :] = m
        taken = work == -jnp.inf
        if store_rank is not None:
            store_rank(rank)
        if store_taken is not None:
            store_taken(taken)
        return jnp.sum(jnp.where(taken, 1.0, 0.0), axis=0, keepdims=True)

    n_taken = extract(first_only=False)

    @pl.when(jnp.max(n_taken) > float(PEER_TOPK))
    def _():
        extract(first_only=True)


def _peer_route_kernel(h_ref, wqh_ref, wql_ref, wsh_ref, wsl_ref,
                       rank2_ref, r_ref, e2_ref, e1n_ref, a_sc, b_sc, c_sc, top_sc, rank1_sc):
    tm = h_ref.shape[0]
    hh, hl = _split_hi_lo(h_ref[...])
    qt = _dot_nt(wqh_ref[...], hh) + (_dot_nt(wql_ref[...], hh) + _dot_nt(wqh_ref[...], hl))
    for h in range(PEER_HEADS):
        qh, ql = _split_hi_lo(qt[h * PEER_QDIM:(h + 1) * PEER_QDIM, :])
        st = _dot(wsh_ref[h], qh) + (_dot(wsl_ref[h], qh) + _dot(wsh_ref[h], ql))
        s1, s2 = st[:N_KEYS], st[N_KEYS:]

        def store_rank1(rank):
            rank1_sc[...] = rank

        def store_rank2(rank, h=h):
            rank2_ref[h] = rank.astype(BF16)

        _top16_rows(s1, a_sc, store_rank=store_rank1)
        _top16_rows(s2, b_sc, store_rank=store_rank2)
        a16, b16 = a_sc[...], b_sc[...]
        c_sc[0:16, :] = a16[0:1] + b16
        for ia in range(1, 8):
            c_sc[8 + 8 * ia:16 + 8 * ia, :] = a16[ia:ia + 1] + b16[0:8]
        c_sc[72:80, :] = a16[8:16] + b16[0:1]
        _top16_rows(c_sc[...], top_sc)
        top = top_sc[...]
        tau = top[PEER_TOPK - 1:PEER_TOPK]
        z = jnp.sum(jnp.exp(top - top[0:1]), axis=0, keepdims=True)
        above, equal = [], []
        for a in range(PEER_TOPK):
            sums = a16[a:a + 1] + b16
            above.append(jnp.sum(jnp.where(sums > tau, 1.0, 0.0), axis=0, keepdims=True))
            equal.append(jnp.sum(jnp.where(sums == tau, 1.0, 0.0), axis=0, keepdims=True))
        need = float(PEER_TOPK) - sum(above)
        rank1 = rank1_sc[...]
        r = jnp.zeros((N_KEYS, tm), F32)
        for a in range(PEER_TOPK):
            take = jnp.minimum(jnp.maximum(need, 0.0), equal[a])
            need = need - take
            r = jnp.where(rank1 == float(a), above[a] + take, r)
        r_ref[h] = r
        e2_ref[h] = jnp.exp(s2 - b16[0:1]).astype(BF16)
        e1n_ref[h] = jnp.exp(s1 - a16[0:1]) / z


def _peer_route(h2d, wq_t, wsub_t, tm):
    t_len = h2d.shape[0]
    wqh, wql = _split_hi_lo(wq_t)
    wsh, wsl = _split_hi_lo(wsub_t)
    full = lambda shape: pl.BlockSpec(shape, lambda i: (0,) * len(shape))
    rows_spec = pl.BlockSpec((PEER_HEADS, N_KEYS, tm), lambda i: (0, 0, i))
    rows_shape = jax.ShapeDtypeStruct((PEER_HEADS, N_KEYS, t_len), F32)
    tile_spec = rows_spec
    tile_shape = jax.ShapeDtypeStruct((PEER_HEADS, N_KEYS, t_len), BF16)
    return pl.pallas_call(
        _peer_route_kernel,
        grid=(t_len // tm,),
        in_specs=[pl.BlockSpec((tm, D_MODEL), lambda i: (i, 0)),
                  full(wqh.shape), full(wql.shape), full(wsh.shape), full(wsl.shape)],
        out_specs=[tile_spec, rows_spec, tile_spec, rows_spec],
        out_shape=[tile_shape, rows_shape, tile_shape, rows_shape],
        scratch_shapes=[pltpu.VMEM((PEER_TOPK, tm), F32), pltpu.VMEM((PEER_TOPK, tm), F32),
                        pltpu.VMEM((80, tm), F32), pltpu.VMEM((PEER_TOPK, tm), F32),
                        pltpu.VMEM((N_KEYS, tm), F32)],
        compiler_params=_params("arbitrary"),
        name="peer_route",
    )(h2d, wqh, wql, wsh, wsl)


def _packed_rows(row):
    return jnp.broadcast_to(row, (BF16_ROWS, row.shape[1])).astype(BF16)


def _peer_dense_kernel(h_ref, u_ref, vt_ref, rank2_ref, r_ref, e2_ref, e1n_ref, g_ref, b_ref,
                       o_ref, xb_sc, ht_sc, g_sc, acc_sc, *, te):
    j = pl.program_id(1)
    tm = h_ref.shape[0]
    rows_per_step = te // N_KEYS

    @pl.when(j == 0)
    def _():
        acc_sc[...] = jnp.zeros(acc_sc.shape, F32)
        xb_sc[...] = h_ref[...].astype(BF16)

    for a in range(rows_per_step):
        i1 = j * rows_per_step + a
        if a % PEER_ROWS_PER_MATMUL == 0:
            ms = slice(a * N_KEYS, (a + PEER_ROWS_PER_MATMUL) * N_KEYS)
            ht_sc[ms, :] = _dot_nt(u_ref[ms, :], xb_sc[...])
        r_rows = [_packed_rows(r_ref[h, pl.ds(i1, 1), :]) for h in range(PEER_HEADS)]
        w_rows = [_packed_rows(e1n_ref[h, pl.ds(i1, 1), :]) for h in range(PEER_HEADS)]
        tiles = []
        for c in range(tm // BF16_MIN_LANES):
            cs = slice(c * BF16_MIN_LANES, (c + 1) * BF16_MIN_LANES)
            gsum = None
            for h in range(PEER_HEADS):
                r_tile = jnp.tile(r_rows[h][:, cs], (N_KEYS // BF16_ROWS, 1))
                w_tile = jnp.tile(w_rows[h][:, cs], (N_KEYS // BF16_ROWS, 1))
                term = jnp.where(rank2_ref[h, :, cs] < r_tile, e2_ref[h, :, cs] * w_tile, 0.0)
                gsum = term if gsum is None else gsum + term
            tiles.append(gsum)
        g_sc[a * N_KEYS:(a + 1) * N_KEYS, :] = jnp.concatenate(tiles, axis=1)
    half = te // 2
    parts = []
    for hs in (slice(0, half), slice(half, te)):
        ht = ht_sc[hs, :]
        act = 0.5 * ht * (1.0 + lax.erf(ht * float(np.sqrt(0.5))))
        parts.append(_dot(vt_ref[:, hs], g_sc[hs, :] * act.astype(BF16)))
    acc_sc[...] += parts[0] + parts[1]

    @pl.when(j == pl.num_programs(1) - 1)
    def _():
        r = DEEPNORM_ALPHA * h_ref[...] + acc_sc[...].T
        o_ref[...] = _layer_norm(r, g_ref[...], b_ref[...])


def _peer_dense(h2d, u_bf, vt_bf, rank2, r, e2, e1n, g, b, tm, te):
    t_len = h2d.shape[0]
    rspec = pl.BlockSpec((PEER_HEADS, N_KEYS, tm), lambda i, j: (0, 0, i))
    tspec = rspec
    vec = pl.BlockSpec((1, D_MODEL), lambda i, j: (0, 0))
    return pl.pallas_call(
        functools.partial(_peer_dense_kernel, te=te),
        grid=(t_len // tm, N_EXPERTS // te),
        in_specs=[pl.BlockSpec((tm, D_MODEL), lambda i, j: (i, 0)),
                  pl.BlockSpec((te, D_MODEL), lambda i, j: (j, 0)),
                  pl.BlockSpec((D_MODEL, te), lambda i, j: (0, j)),
                  tspec, rspec, tspec, rspec, vec, vec],
        out_specs=pl.BlockSpec((tm, D_MODEL), lambda i, j: (i, 0)),
        out_shape=jax.ShapeDtypeStruct((t_len, D_MODEL), F32),
        scratch_shapes=[pltpu.VMEM((tm, D_MODEL), BF16), pltpu.VMEM((te, tm), F32),
                        pltpu.VMEM((te, tm), BF16), pltpu.VMEM((D_MODEL, tm), F32)],
        compiler_params=_params("arbitrary", "arbitrary"),
        name="peer_dense",
    )(h2d, u_bf, vt_bf, rank2, r, e2, e1n, g.reshape(1, D_MODEL), b.reshape(1, D_MODEL))


def _sub_key_blocks(subkeys):
    hn, _, nk, d = subkeys.shape
    z = jnp.zeros((hn, nk, d), subkeys.dtype)
    top = jnp.concatenate([subkeys[:, 0], z], axis=-1)
    bot = jnp.concatenate([z, subkeys[:, 1]], axis=-1)
    return jnp.concatenate([top, bot], axis=1)


_PROJ_GROUPS = (
    ("cols", "bf16"), ("rows", "bf16"), ("coltiles", "bf16"),
    ("cols", "hilo"), ("rows", "hilo"), ("cols", "f32"),
    ("heads", "bf16"), ("heads", "bf16"), ("heads", "bf16"),
    ("cols", "hilo"), ("heads", "f32"), ("coltiles", "bf16"),
    ("cols", "bf16"), ("heads", "bf16"), ("coltiles", "bf16"),
)


def _token_mixer(h2d, bn, s_len, w_in, w_branch, w_out, sinks, ln_g, ln_b, slopes):
    offs = np.concatenate([[0], np.cumsum(IN_WIDTHS)])
    cols = [w_in[:, offs[k]:offs[k + 1]] for k in range(len(IN_WIDTHS))]
    groups = [(cols[k], layout, prec) for k, (layout, prec) in enumerate(_PROJ_GROUPS)]
    (qa, ka, va, qi_h, qi_l, ki_h, ki_l, wi, qb, kb, vb, qc_h, qc_l, kc, vc, qd, kd, vd) = _project(
        h2d, groups, tm=PROJ_TM, kt=ATT_TILE)
    k3 = jnp.concatenate([ki_h, ki_h, ki_l], axis=-1)
    yts = [_dsa(qi_h, qi_l, wi, k3, qa, ka, va, bn, s_len, slopes[0]),
           _swa(sinks, qb, kb, vb, bn, s_len, slopes[1]),
           _moba(qc_h, qc_l, kc, vc, bn, s_len, slopes[2]),
           _stick(qd, kd, vd, bn, s_len)]
    return _merge(h2d, yts, cols[15].astype(BF16), w_branch.astype(BF16), w_out.astype(BF16),
                  ln_g, ln_b, tm=MERGE_TM)


def _peer(h2d, wq, subkeys, u, v, ln_g, ln_b):
    rank2, r, e2, e1n = _peer_route(h2d, wq.T, _sub_key_blocks(subkeys), tm=ROUTE_TM)
    return _peer_dense(h2d, u.astype(BF16), v.T.astype(BF16), rank2, r, e2, e1n, ln_g, ln_b,
                       tm=PEER_TM, te=PEER_TE)


def kernel(x, w_in, w_branch, w_out, attn_sinks, ln1_g, ln1_b, peer_wq, peer_subkeys,
           peer_u, peer_v, ln2_g, ln2_b):
    bn, s_len, d = x.shape
    slopes = _alibi_slopes()
    h = x.reshape(bn * s_len, d)
    for l in range(DEPTH):
        h = _token_mixer(h, bn, s_len, w_in[l], w_branch[l], w_out[l], attn_sinks[l],
                         ln1_g[l], ln1_b[l], slopes)
        h = _peer(h, peer_wq[l], peer_subkeys[l], peer_u[l], peer_v[l], ln2_g[l], ln2_b[l])
    return h.reshape(bn, s_len, d)
```
